```python
import math, functools
import jax, jax.numpy as jnp
from jax import lax
import numpy as np

D_MODEL = 1024
BATCH = 1
SEQ = 16384
DEPTH = 2

GRID_W = 64
CTX_LEN = 256
EPS = 1e-6

HY_WIDTH = 256
HY_SHORT = 3
HY_ORDER = 2
HY_BANDS = 16
HY_EMB = 1 + 2 * HY_BANDS
HY_HIDDEN = 64
HY_TARGET = 1e-2
HY_FAST_DECAY = 0.3
HY_SLOW_DECAY = 1.5
HY_MIN_DECAY = math.log(HY_TARGET) / HY_SLOW_DECAY
HY_MAX_DECAY = math.log(HY_TARGET) / HY_FAST_DECAY
HY_FILTER_STD = 0.005
HY_COLS = 3 * HY_WIDTH

SSD_HEADS = 8
SSD_HEAD_DIM = 64
SSD_WIDTH = SSD_HEADS * SSD_HEAD_DIM
SSD_GROUPS = 2
SSD_HPG = SSD_HEADS // SSD_GROUPS
SSD_STATE = 128
SSD_CONV = 3
SSD_CHUNK = 128
SSD_XBC = SSD_WIDTH + 2 * SSD_GROUPS * SSD_STATE
SSD_COLS = SSD_WIDTH + SSD_XBC + 2 * SSD_HEADS

ATTN_HEADS = 4
ATTN_KV_HEADS = 2
ATTN_REP = ATTN_HEADS // ATTN_KV_HEADS
ATTN_HEAD_DIM = 64
ATTN_WIDTH = ATTN_HEADS * ATTN_HEAD_DIM
ATTN_KV_WIDTH = ATTN_KV_HEADS * ATTN_HEAD_DIM
ATTN_BLOCK = 128
ROPE_BASE = 10000.0
ATTN_COLS = ATTN_WIDTH + 2 * ATTN_KV_WIDTH

MIX_WIDTH = HY_WIDTH + SSD_WIDTH + ATTN_WIDTH
D_IN = HY_COLS + SSD_COLS + ATTN_COLS

D_FF = 2816
FFN_CONV = 3

kernel_name = 'hybrid_hyena_ssd_gqa_dit_trunk'


def _layernorm(x, g, b):
    xf = x.astype(jnp.float32)
    mu = jnp.mean(xf, axis=-1, keepdims=True)
    var = jnp.mean(jnp.square(xf - mu), axis=-1, keepdims=True)
    return ((xf - mu) * lax.rsqrt(var + EPS) * g + b).astype(x.dtype)


def _rmsnorm(x, g):
    xf = x.astype(jnp.float32)
    return (xf * lax.rsqrt(jnp.mean(xf * xf, axis=-1, keepdims=True) + EPS) * g).astype(x.dtype)


def _dwconv(x, w, b):
    k = w.shape[0]
    y = lax.conv_general_dilated(x, w[:, None, :].astype(x.dtype), window_strides=(1,),
                                 padding=[(k // 2, k // 2)],
                                 dimension_numbers=('NWC', 'WIO', 'NWC'),
                                 feature_group_count=x.shape[-1])
    return y + b


def _flip(t):
    return jnp.flip(t, axis=1)


def _grid_rope(n_tokens):
    rows = n_tokens // GRID_W
    row = jnp.repeat(jnp.arange(rows, dtype=jnp.float32), GRID_W)
    col = jnp.tile(jnp.arange(GRID_W, dtype=jnp.float32), rows)
    n_freq = ATTN_HEAD_DIM // 4
    inv = ROPE_BASE ** (-jnp.arange(n_freq, dtype=jnp.float32) / n_freq)
    ang = jnp.concatenate([row[:, None] * inv, col[:, None] * inv], axis=-1)
    return jnp.cos(ang), jnp.sin(ang)


def _apply_rope(x, cos, sin):
    x1, x2 = jnp.split(x.astype(jnp.float32), 2, axis=-1)
    c = cos[None, :, None, :]
    s = sin[None, :, None, :]
    return jnp.concatenate([x1 * c - x2 * s, x1 * s + x2 * c], axis=-1).astype(x.dtype)


def _split_cols(p):
    return (p[..., :HY_COLS], p[..., HY_COLS:HY_COLS + SSD_COLS], p[..., HY_COLS + SSD_COLS:])


def _hyena_filters(n, w1, b1, w2, b2, w3, freq):
    t = jnp.linspace(0.0, 1.0, n, dtype=jnp.float32)[:, None]
    w = (2.0 * math.pi / n) * jnp.arange(n, dtype=jnp.float32)[:, None]
    f = jnp.linspace(1e-4, HY_BANDS - 1, HY_BANDS, dtype=jnp.float32)[None, :]
    feats = jnp.concatenate([t, jnp.cos(f * w), -jnp.sin(f * w)], axis=-1)
    h = jnp.sin(freq * (feats @ w1 + b1))
    h = jnp.sin(freq * (h @ w2 + b2))
    h = (h @ w3).astype(jnp.float32)
    deltas = jnp.abs(jnp.linspace(HY_MIN_DECAY, HY_MAX_DECAY, HY_WIDTH, dtype=jnp.float32))
    window = jnp.exp(-t * deltas)
    return h.reshape(n, HY_ORDER, 2, HY_WIDTH) * window[:, None, None, :]


def _bidir_longconv(u, h_f, h_b, skip):
    n = u.shape[1]
    k = jnp.concatenate([h_f, jnp.zeros_like(h_f[:1]), h_b[:0:-1]], axis=0)
    kf = jnp.fft.rfft(k, axis=0)
    uf = jnp.fft.rfft(u.astype(jnp.float32), n=2 * n, axis=1)
    y = jnp.fft.irfft(uf * kf[None], n=2 * n, axis=1)[:, :n]
    return (y + u.astype(jnp.float32) * skip.astype(jnp.float32)).astype(u.dtype)


def _hyena(u, filt, conv_w, conv_b, skip):
    u = _dwconv(u, conv_w, conv_b)
    v, x1, x2 = jnp.split(u, 3, axis=-1)
    z = x1 * _bidir_longconv(v, filt[:, 0, 0], filt[:, 0, 1], skip[0])
    return x2 * _bidir_longconv(z, filt[:, 1, 0], filt[:, 1, 1], skip[1])


def _ssd_prep(u, conv_w, conv_b, dt_bias):
    b, n = u.shape[:2]
    z = u[..., :SSD_WIDTH]
    xbc = jax.nn.silu(_dwconv(u[..., SSD_WIDTH:SSD_WIDTH + SSD_XBC], conv_w, conv_b))
    dt_raw = u[..., SSD_WIDTH + SSD_XBC:].astype(jnp.float32)
    xs = xbc[..., :SSD_WIDTH].reshape(b, n, SSD_GROUPS, SSD_HPG, SSD_HEAD_DIM)
    bm = xbc[..., SSD_WIDTH:SSD_WIDTH + SSD_GROUPS * SSD_STATE].reshape(b, n, SSD_GROUPS, SSD_STATE)
    cm = xbc[..., SSD_WIDTH + SSD_GROUPS * SSD_STATE:].reshape(b, n, SSD_GROUPS, SSD_STATE)
    dt = jax.nn.softplus(dt_raw.reshape(b, n, 2, SSD_GROUPS, SSD_HPG)
                         + dt_bias.astype(jnp.float32).reshape(2, SSD_GROUPS, SSD_HPG))
    return z, xs, bm, cm, dt


def _chunk(t):
    return t.reshape(t.shape[0], t.shape[1] // SSD_CHUNK, SSD_CHUNK, *t.shape[2:])


def _ssd_states(xc, dtc, a_cum, bc):
    decay = jnp.exp(a_cum[:, :, -1:] - a_cum)
    return jnp.einsum('bclgn,bclgr,bclgrp->bcgrpn', bc, decay * dtc, xc)


def _ssd_final_state(xs, dt, a, bm):
    a_cum = jnp.cumsum(dt * a, axis=1)[:, None]
    return _ssd_states(xs[:, None], dt[:, None], a_cum, bm[:, None])[:, 0]


def _ssd_scan(xs, dt, a, bm, cm, h0):
    xc, dtc, bc, cc = _chunk(xs), _chunk(dt), _chunk(bm), _chunk(cm)
    a_cum = jnp.cumsum(dtc * a, axis=2)
    states = _ssd_states(xc, dtc, a_cum, bc)
    seg = a_cum[:, :, :, None] - a_cum[:, :, None, :]
    lower = jnp.tril(jnp.ones((SSD_CHUNK, SSD_CHUNK), dtype=bool))[:, :, None, None]
    decay = jnp.exp(jnp.where(lower, seg, -jnp.inf))
    scores = jnp.einsum('bclgn,bcsgn->bclsg', cc, bc)
    y = jnp.einsum('bclsgr,bcsgrp->bclgrp', scores[..., None] * decay, xc * dtc[..., None])

    def step(h, inp):
        st, dec = inp
        return h * dec[..., None, None] + st, h

    h_last, h_prev = lax.scan(step, h0, (jnp.moveaxis(states, 1, 0),
                                         jnp.moveaxis(jnp.exp(a_cum[:, :, -1]), 1, 0)))
    h_prev = jnp.moveaxis(h_prev, 0, 1)
    y = y + jnp.einsum('bclgn,bcgrpn->bclgrp', cc, h_prev) * jnp.exp(a_cum)[..., None]
    return y.reshape(xs.shape), h_last


def _ssd_output(y_f, y_b, xs, z, d, norm_w):
    b, n = xs.shape[:2]
    y = y_f + y_b + d.astype(jnp.float32).reshape(SSD_GROUPS, SSD_HPG, 1) * xs
    gated = y.reshape(b, n, SSD_WIDTH) * jax.nn.silu(z.astype(jnp.float32))
    out = _rmsnorm(gated.reshape(b, n, SSD_GROUPS, SSD_WIDTH // SSD_GROUPS),
                   norm_w.reshape(SSD_GROUPS, SSD_WIDTH // SSD_GROUPS))
    return out.reshape(b, n, SSD_WIDTH).astype(z.dtype)


def _heads(t, n_heads, g):
    return _rmsnorm(t.reshape(t.shape[0], t.shape[1], n_heads, ATTN_HEAD_DIM), g)


def _block_attention(q, k, v):
    b, n = q.shape[:2]
    nb = n // ATTN_BLOCK
    qb = jnp.moveaxis(q.reshape(b, nb, ATTN_BLOCK, ATTN_KV_HEADS, ATTN_REP, ATTN_HEAD_DIM), 1, 0)
    scale = ATTN_HEAD_DIM ** -0.5

    def one(qblk):
        s = jnp.einsum('bqgrd,bkgd->bgrqk', qblk, k).astype(jnp.float32) * scale
        p = jax.nn.softmax(s, axis=-1).astype(v.dtype)
        return jnp.einsum('bgrqk,bkgd->bqgrd', p, v)

    o = lax.map(one, qb)
    return jnp.moveaxis(o, 0, 1).reshape(b, n, ATTN_WIDTH)


def _mixer(h_lat, h_ctx, ctx_out, rope, w_in, hy_conv_w, hy_conv_b, hy_ffn_w1, hy_ffn_b1,
           hy_ffn_w2, hy_ffn_b2, hy_ffn_w3, hy_freq, hy_bias, ssd_conv_w, ssd_conv_b,
           ssd_dt_bias, ssd_a_log, ssd_d, ssd_norm_w, attn_q_norm, attn_k_norm, w_out):
    filt = functools.partial(_hyena_filters, w1=hy_ffn_w1, b1=hy_ffn_b1, w2=hy_ffn_w2,
                             b2=hy_ffn_b2, w3=hy_ffn_w3, freq=hy_freq)
    hy_l, ssd_l, at_l = _split_cols(h_lat @ w_in)
    hy_c, ssd_c, at_c = _split_cols(h_ctx @ w_in)

    y_hy_l = _hyena(hy_l, filt(h_lat.shape[1]), hy_conv_w, hy_conv_b, hy_bias)

    a = -jnp.exp(ssd_a_log.astype(jnp.float32)).reshape(2, SSD_GROUPS, SSD_HPG)
    z_l, xs_l, b_l, c_l, dt_l = _ssd_prep(ssd_l, ssd_conv_w, ssd_conv_b, ssd_dt_bias)
    z_c, xs_c, b_c, c_c, dt_c = _ssd_prep(ssd_c, ssd_conv_w, ssd_conv_b, ssd_dt_bias)
    if ctx_out:
        h0 = jnp.zeros((h_ctx.shape[0], SSD_GROUPS, SSD_HPG, SSD_HEAD_DIM, SSD_STATE), jnp.float32)
        yc_f, hc_f = _ssd_scan(xs_c, dt_c[:, :, 0], a[0], b_c, c_c, h0)
        yc_b, hc_b = _ssd_scan(_flip(xs_c), _flip(dt_c[:, :, 1]), a[1], _flip(b_c), _flip(c_c), h0)
        y_ssd_c = _ssd_output(yc_f, _flip(yc_b), xs_c, z_c, ssd_d, ssd_norm_w)
    else:
        hc_f = _ssd_final_state(xs_c, dt_c[:, :, 0], a[0], b_c)
        hc_b = _ssd_final_state(_flip(xs_c), _flip(dt_c[:, :, 1]), a[1], _flip(b_c))
    yl_f, _ = _ssd_scan(xs_l, dt_l[:, :, 0], a[0], b_l, c_l, hc_f)
    yl_b, _ = _ssd_scan(_flip(xs_l), _flip(dt_l[:, :, 1]), a[1], _flip(b_l), _flip(c_l), hc_b)
    y_ssd_l = _ssd_output(yl_f, _flip(yl_b), xs_l, z_l, ssd_d, ssd_norm_w)

    q_l = _apply_rope(_heads(at_l[..., :ATTN_WIDTH], ATTN_HEADS, attn_q_norm), *rope)
    k_l = _apply_rope(_heads(at_l[..., ATTN_WIDTH:ATTN_WIDTH + ATTN_KV_WIDTH], ATTN_KV_HEADS, attn_k_norm), *rope)
    v_l = at_l[..., ATTN_WIDTH + ATTN_KV_WIDTH:].reshape(*at_l.shape[:2], ATTN_KV_HEADS, ATTN_HEAD_DIM)
    k_c = _heads(at_c[..., ATTN_WIDTH:ATTN_WIDTH + ATTN_KV_WIDTH], ATTN_KV_HEADS, attn_k_norm)
    v_c = at_c[..., ATTN_WIDTH + ATTN_KV_WIDTH:].reshape(*at_c.shape[:2], ATTN_KV_HEADS, ATTN_HEAD_DIM)
    y_at_l = _block_attention(q_l, jnp.concatenate([k_c, k_l], axis=1), jnp.concatenate([v_c, v_l], axis=1))

    m_lat = jnp.concatenate([y_hy_l, y_ssd_l, y_at_l], axis=-1) @ w_out
    if not ctx_out:
        return m_lat, None
    y_hy_c = _hyena(hy_c, filt(h_ctx.shape[1]), hy_conv_w, hy_conv_b, hy_bias)
    q_c = _heads(at_c[..., :ATTN_WIDTH], ATTN_HEADS, attn_q_norm)
    y_at_c = _block_attention(q_c, k_c, v_c)
    m_ctx = jnp.concatenate([y_hy_c, y_ssd_c, y_at_c], axis=-1) @ w_out
    return m_lat, m_ctx


def _conv_ffn(h, w_up, conv_w, conv_b, w_down):
    u = _dwconv(h @ w_up, conv_w, conv_b)
    g, val = jnp.split(u, 2, axis=-1)
    return (jax.nn.silu(g) * val) @ w_down


def setup_inputs(seed: int = 0) -> dict:
    key = jax.random.key(seed)
    ks = iter(jax.random.split(key, 40))

    def nrm(shape, std):
        return std * jax.random.normal(next(ks), shape, jnp.float32)

    beta = (8.0 * DEPTH) ** -0.25
    L = DEPTH
    dt0 = jnp.exp(jax.random.uniform(next(ks), (L, 2, SSD_HEADS), jnp.float32,
                                     math.log(1e-3), math.log(1e-1)))
    return {
        'x': nrm((BATCH, SEQ, D_MODEL), 1.0),
        'c': nrm((BATCH, D_MODEL), 1.0),
        'ctx': nrm((BATCH, CTX_LEN, D_MODEL), 1.0),
        'c_ctx': nrm((D_MODEL,), 1.0),
        'w_mod': nrm((L, D_MODEL, 6 * D_MODEL), D_MODEL ** -0.5),
        'b_mod': nrm((L, 6 * D_MODEL), 0.02),
        'w_in': nrm((L, D_MODEL, D_IN), D_MODEL ** -0.5),
        'hy_conv_w': nrm((L, HY_SHORT, HY_COLS), HY_SHORT ** -0.5),
        'hy_conv_b': nrm((L, HY_COLS), 0.02),
        'hy_ffn_w1': nrm((L, HY_EMB, HY_HIDDEN), HY_EMB ** -0.5),
        'hy_ffn_b1': nrm((L, HY_HIDDEN), 0.02),
        'hy_ffn_w2': nrm((L, HY_HIDDEN, HY_HIDDEN), HY_HIDDEN ** -0.5),
        'hy_ffn_b2': nrm((L, HY_HIDDEN), 0.02),
        'hy_ffn_w3': nrm((L, HY_HIDDEN, HY_ORDER * 2 * HY_WIDTH), HY_FILTER_STD),
        'hy_freq': 1.0 + nrm((L, HY_HIDDEN), 0.1),
        'hy_bias': nrm((L, HY_ORDER, HY_WIDTH), 0.5),
        'ssd_conv_w': nrm((L, SSD_CONV, SSD_XBC), SSD_CONV ** -0.5),
        'ssd_conv_b': nrm((L, SSD_XBC), 0.02),
        'ssd_dt_bias': dt0 + jnp.log(-jnp.expm1(-dt0)),
        'ssd_a_log': jnp.log(jax.random.uniform(next(ks), (L, 2, SSD_HEADS), jnp.float32, 1.0, 16.0)),
        'ssd_d': 1.0 + nrm((L, SSD_HEADS), 0.1),
        'ssd_norm_w': 1.0 + nrm((L, SSD_WIDTH), 0.02),
        'attn_q_norm': 1.0 + nrm((L, ATTN_HEAD_DIM), 0.02),
        'attn_k_norm': 1.0 + nrm((L, ATTN_HEAD_DIM), 0.02),
        'w_out': nrm((L, MIX_WIDTH, D_MODEL), beta * MIX_WIDTH ** -0.5),
        'ln1_g': 1.0 + nrm((L, D_MODEL), 0.02),
        'ln1_b': nrm((L, D_MODEL), 0.02),
        'ffn_w_up': nrm((L, D_MODEL, 2 * D_FF), D_MODEL ** -0.5),
        'ffn_conv_w': nrm((L, FFN_CONV, 2 * D_FF), FFN_CONV ** -0.5),
        'ffn_conv_b': nrm((L, 2 * D_FF), 0.02),
        'ffn_w_down': nrm((L, D_FF, D_MODEL), beta * D_FF ** -0.5),
        'ln2_g': 1.0 + nrm((L, D_MODEL), 0.02),
        'ln2_b': nrm((L, D_MODEL), 0.02),
    }


def reference(x, c, ctx, c_ctx, w_mod, b_mod, w_in, hy_conv_w, hy_conv_b, hy_ffn_w1, hy_ffn_b1,
              hy_ffn_w2, hy_ffn_b2, hy_ffn_w3, hy_freq, hy_bias, ssd_conv_w, ssd_conv_b,
              ssd_dt_bias, ssd_a_log, ssd_d, ssd_norm_w, attn_q_norm, attn_k_norm, w_out,
              ln1_g, ln1_b, ffn_w_up, ffn_conv_w, ffn_conv_b, ffn_w_down, ln2_g, ln2_b):
    alpha = (2.0 * DEPTH) ** 0.25
    rope = _grid_rope(x.shape[1])
    s_lat = jax.nn.silu(c)
    s_ctx = jax.nn.silu(c_ctx)
    x_lat, x_ctx = x, ctx
    for i in range(DEPTH):
        ctx_out = i < DEPTH - 1
        sh1, sc1, g1, sh2, sc2, g2 = jnp.split((s_lat @ w_mod[i] + b_mod[i])[:, None, :], 6, axis=-1)
        ch1, cs1, cg1, ch2, cs2, cg2 = jnp.split(s_ctx @ w_mod[i] + b_mod[i], 6, axis=-1)
        m_lat, m_ctx = _mixer(x_lat * (1.0 + sc1) + sh1, x_ctx * (1.0 + cs1) + ch1, ctx_out, rope,
                              w_in[i], hy_conv_w[i], hy_conv_b[i], hy_ffn_w1[i], hy_ffn_b1[i],
                              hy_ffn_w2[i], hy_ffn_b2[i], hy_ffn_w3[i], hy_freq[i], hy_bias[i],
                              ssd_conv_w[i], ssd_conv_b[i], ssd_dt_bias[i], ssd_a_log[i], ssd_d[i],
                              ssd_norm_w[i], attn_q_norm[i], attn_k_norm[i], w_out[i])
        x_lat = _layernorm(alpha * x_lat + g1 * m_lat, ln1_g[i], ln1_b[i])
        f_lat = _conv_ffn(x_lat * (1.0 + sc2) + sh2, ffn_w_up[i], ffn_conv_w[i], ffn_conv_b[i], ffn_w_down[i])
        x_lat = _layernorm(alpha * x_lat + g2 * f_lat, ln2_g[i], ln2_b[i])
        if ctx_out:
            x_ctx = _layernorm(alpha * x_ctx + cg1 * m_ctx, ln1_g[i], ln1_b[i])
            f_ctx = _conv_ffn(x_ctx * (1.0 + cs2) + ch2, ffn_w_up[i], ffn_conv_w[i], ffn_conv_b[i], ffn_w_down[i])
            x_ctx = _layernorm(alpha * x_ctx + cg2 * f_ctx, ln2_g[i], ln2_b[i])
    return x_lat
```

```python
import functools
import math

import numpy as np
import jax
import jax.numpy as jnp
from jax import lax
from jax.experimental import pallas as pl
from jax.experimental.pallas import tpu as pltpu

F32 = jnp.float32
BF16 = jnp.bfloat16
HI = lax.Precision.HIGHEST

D_MODEL = 1024
DEPTH = 2
GRID_W = 64
CTX = 256
EPS = 1e-6

HY_W = 256
HY_BANDS = 16
HY_EMB = 1 + 2 * HY_BANDS
HY_HID = 64
HY_MIN_DECAY = math.log(1e-2) / 1.5
HY_MAX_DECAY = math.log(1e-2) / 0.3
HY_COLS = 3 * HY_W

SSD_H = 8
SSD_P = 64
SSD_W = SSD_H * SSD_P
SSD_G = 2
SSD_R = SSD_H // SSD_G
SSD_N = 128
SSD_XBC = SSD_W + 2 * SSD_G * SSD_N
CHUNK = 128

AT_H = 4
AT_KV = 2
AT_D = 64
AT_W = AT_H * AT_D
AT_KVW = AT_KV * AT_D
ROPE_BASE = 10000.0

MIX_W = HY_W + SSD_W + AT_W
D_FF = 2816
FF_CHUNK = 256

TM = CTX
LANES = 128
N2 = 256
V7X_VMEM_LIMIT = 56 * 1024 * 1024


def _cparams(sem, vmem=None):
    return pltpu.CompilerParams(dimension_semantics=sem, vmem_limit_bytes=vmem)


def _dot(a, b):
    return jnp.dot(a.astype(BF16), b.astype(BF16), preferred_element_type=F32)


def _dot_hi(a, b):
    return jnp.dot(a, b, preferred_element_type=F32, precision=HI)


def _sigmoid(x):
    return 1.0 / (1.0 + jnp.exp(-x))


def _silu(x):
    return x * _sigmoid(x)


def _full(shape):
    nd = len(shape)
    return pl.BlockSpec(shape, lambda *_: (0,) * nd)


def _mod_body(c_ref, w_ref, b_ref, o_ref):
    s = _silu(c_ref[...])
    o_ref[0] = _dot(s, w_ref[0]) + b_ref[0]


def _modulation(cvec, w_mod, b_mod):
    depth, d, n = w_mod.shape
    nb = 1536
    return pl.pallas_call(
        _mod_body,
        grid=(depth, n // nb),
        in_specs=[pl.BlockSpec((8, d), lambda l, j: (0, 0)),
                  pl.BlockSpec((1, d, nb), lambda l, j: (l, 0, j)),
                  pl.BlockSpec((1, 1, nb), lambda l, j: (l, 0, j))],
        out_specs=pl.BlockSpec((1, 8, nb), lambda l, j: (l, 0, j)),
        out_shape=jax.ShapeDtypeStruct((depth, 8, n), F32),
        compiler_params=_cparams(("arbitrary", "arbitrary")),
        name="modulation",
    )(cvec, w_mod, b_mod.reshape(depth, 1, n))


W_MAIN = SSD_W + SSD_XBC + LANES + AT_W + 2 * AT_KVW


def _inproj_body(x_ref, sc_ref, sh_ref, wm_ref, wh_ref, hy_ref, z_ref, xbc_ref, dt_ref, qkv_ref):
    h = (x_ref[...] * (1.0 + sc_ref[0]) + sh_ref[0]).astype(BF16)
    main = jnp.dot(h, wm_ref[...], preferred_element_type=F32)
    z_ref[...] = main[:, 0:SSD_W]
    xbc_ref[...] = main[:, SSD_W:SSD_W + SSD_XBC]
    o = SSD_W + SSD_XBC
    dt_ref[...] = main[:, o:o + LANES]
    qkv_ref[...] = main[:, o + LANES:]
    hy_ref[...] = lax.dot_general(wh_ref[...], h, (((1,), (1,)), ((), ())),
                                  preferred_element_type=F32)


def _in_proj(x_tok, sc, sh, w_main, w_hyT):
    t = x_tok.shape[0]
    nt = t // TM
    sel = lambda i: (jnp.where(i == nt - 1, 1, 0), 0, 0)
    row = lambda i: (i, 0)
    return pl.pallas_call(
        _inproj_body,
        grid=(nt,),
        in_specs=[pl.BlockSpec((TM, D_MODEL), row),
                  pl.BlockSpec((1, 1, D_MODEL), sel),
                  pl.BlockSpec((1, 1, D_MODEL), sel),
                  _full((D_MODEL, W_MAIN)),
                  _full((HY_COLS, D_MODEL))],
        out_specs=[pl.BlockSpec((HY_COLS, TM), lambda i: (0, i)),
                   pl.BlockSpec((TM, SSD_W), row),
                   pl.BlockSpec((TM, SSD_XBC), row),
                   pl.BlockSpec((TM, LANES), row),
                   pl.BlockSpec((TM, AT_W + 2 * AT_KVW), row)],
        out_shape=[jax.ShapeDtypeStruct((HY_COLS, t), F32),
                   jax.ShapeDtypeStruct((t, SSD_W), F32),
                   jax.ShapeDtypeStruct((t, SSD_XBC), F32),
                   jax.ShapeDtypeStruct((t, LANES), F32),
                   jax.ShapeDtypeStruct((t, AT_W + 2 * AT_KVW), F32)],
        compiler_params=_cparams(("arbitrary",), V7X_VMEM_LIMIT),
        name="in_proj",
    )(x_tok, sc, sh, w_main, w_hyT)


def _filter_body(w1_ref, b1_ref, w2_ref, b2_ref, w3_ref, fr_ref, o_ref, *, n, nt):
    j = pl.program_id(0)
    k = (lax.broadcasted_iota(jnp.int32, (1, nt), 1) + j * nt).astype(F32)
    t = k * (1.0 / (n - 1))
    wk = k * (2.0 * math.pi / n)
    r = lax.broadcasted_iota(jnp.int32, (LANES, nt), 0)
    band = jnp.where(r <= HY_BANDS, r - 1, r - 1 - HY_BANDS).astype(F32)
    f = 1e-4 + band * ((HY_BANDS - 1 - 1e-4) / (HY_BANDS - 1))
    arg = f * wk
    feats = jnp.where(r == 0, t, jnp.where(r <= HY_BANDS, jnp.cos(arg),
                                           jnp.where(r < HY_EMB, -jnp.sin(arg), 0.0)))
    fr = fr_ref[...]
    h = jnp.sin(fr * (_dot_hi(w1_ref[...], feats) + b1_ref[...]))
    h = jnp.sin(fr * (_dot_hi(w2_ref[...], h) + b2_ref[...]))
    h = _dot_hi(w3_ref[...], h)
    c = lax.broadcasted_iota(jnp.int32, (HY_W, nt), 0).astype(F32)
    delta = jnp.abs(HY_MIN_DECAY + c * ((HY_MAX_DECAY - HY_MIN_DECAY) / (HY_W - 1)))
    window = jnp.exp(-t * delta)
    for q in range(4):
        o_ref[q] = h[q * HY_W:(q + 1) * HY_W] * window


def _hyena_filters(n, w1t, b1, w2t, b2, w3t, freq):
    nt = min(n, 2048)
    return pl.pallas_call(
        functools.partial(_filter_body, n=n, nt=nt),
        grid=(n // nt,),
        in_specs=[_full((HY_HID, LANES)), _full((HY_HID, 1)), _full((HY_HID, HY_HID)),
                  _full((HY_HID, 1)), _full((4 * HY_W, HY_HID)), _full((HY_HID, 1))],
        out_specs=pl.BlockSpec((4, HY_W, nt), lambda j: (0, 0, j)),
        out_shape=jax.ShapeDtypeStruct((4, HY_W, n), F32),
        compiler_params=_cparams(("arbitrary",)),
        name="hyena_filters",
    )(w1t, b1, w2t, b2, w3t, freq)


def _fft_tables(n):
    big = 2 * n
    n1 = big // N2
    r1 = n1 // 2
    kp = -(-(r1 + 1) // 8) * 8
    k1 = np.arange(kp)[:, None].astype(np.float64)
    live = (k1 <= r1)
    a1 = 2 * np.pi * k1 * np.arange(r1)[None, :] / n1
    f1a = np.concatenate([np.cos(a1) * live, -np.sin(a1) * live], axis=0)
    at = 2 * np.pi * k1 * np.arange(N2)[None, :] / big
    twr, twi = np.cos(at) * live, -np.sin(at) * live
    a2 = 2 * np.pi * np.outer(np.arange(N2), np.arange(N2)) / N2
    c2, s2 = np.cos(a2), np.sin(a2)
    w2f = np.block([[c2, -s2], [s2, c2]])
    w2i = np.block([[c2, s2], [-s2, c2]])
    wgt = np.where((k1 == 0) | (k1 == r1), 1.0, 2.0) * live / big
    f1i = np.concatenate([(np.cos(a1) * wgt).T, (-np.sin(a1) * wgt).T], axis=1)
    j = jnp.asarray
    return dict(r1=r1, kp=kp, f1a=j(f1a, BF16), twr=j(twr, F32), twi=j(twi, F32),
                w2f=j(w2f, BF16), w2i=j(w2i, BF16), f1i=j(f1i, BF16))


def _fft_rows(x, f1a_ref, twr_ref, twi_ref, kp):
    a = jnp.dot(f1a_ref[...], x.astype(BF16), preferred_element_type=F32)
    ar, ai = a[:kp], a[kp:]
    twr, twi = twr_ref[...], twi_ref[...]
    return jnp.concatenate([ar * twr - ai * twi, ar * twi + ai * twr], axis=1)


def _kspec_body(hf_ref, hb_ref, f1a_ref, twr_ref, twi_ref, w2f_ref, o_ref, sa_ref, *, cb, kp, r1):
    row = lax.broadcasted_iota(jnp.int32, (r1, N2), 0)
    lane = lax.broadcasted_iota(jnp.int32, (r1, N2), 1)
    first = (row == 0) & (lane == 0)

    def chan(c, carry):
        sa_ref[pl.ds(pl.multiple_of(c * kp, 8), kp), :] = _fft_rows(
            hf_ref[0, c], f1a_ref, twr_ref, twi_ref, kp)
        hb = jnp.where(first, 0.0, hb_ref[0, c])
        sa_ref[pl.ds(pl.multiple_of((cb + c) * kp, 8), kp), :] = _fft_rows(
            hb, f1a_ref, twr_ref, twi_ref, kp)
        return carry

    lax.fori_loop(0, cb, chan, 0)
    x = jnp.dot(sa_ref[...].astype(BF16), w2f_ref[...], preferred_element_type=F32)
    xf, xb = x[:cb * kp], x[cb * kp:]
    kf = jnp.concatenate([xf[:, :N2] + xb[:, :N2], xf[:, N2:] - xb[:, N2:]], axis=1)
    o_ref[0] = kf.reshape(cb, kp, 2 * N2)


def _filter_spectra(filt, tb, cb):
    r1, kp = tb["r1"], tb["kp"]
    return pl.pallas_call(
        functools.partial(_kspec_body, cb=cb, kp=kp, r1=r1),
        grid=(2, HY_W // cb),
        in_specs=[pl.BlockSpec((1, cb, r1, N2), lambda o, j: (2 * o, j, 0, 0)),
                  pl.BlockSpec((1, cb, r1, N2), lambda o, j: (2 * o + 1, j, 0, 0)),
                  _full((2 * kp, r1)), _full((kp, N2)), _full((kp, N2)), _full((2 * N2, 2 * N2))],
        out_specs=pl.BlockSpec((1, cb, kp, 2 * N2), lambda o, j: (o, j, 0, 0)),
        out_shape=jax.ShapeDtypeStruct((2, HY_W, kp, 2 * N2), F32),
        scratch_shapes=[pltpu.VMEM((2 * cb * kp, 2 * N2), F32)],
        compiler_params=_cparams(("arbitrary", "arbitrary"), V7X_VMEM_LIMIT),
        name="hyena_filter_spectra",
    )(filt, filt, tb["f1a"], tb["twr"], tb["twi"], tb["w2f"])


def _short_conv(x, w0, w1, w2, b):
    r1 = x.shape[0]
    row = lax.broadcasted_iota(jnp.int32, x.shape, 0)
    lane = lax.broadcasted_iota(jnp.int32, x.shape, 1)
    a = pltpu.roll(x, 1, 1)
    prev = jnp.where(lane == 0, jnp.where(row == 0, 0.0, pltpu.roll(a, 1, 0)), a)
    a = pltpu.roll(x, N2 - 1, 1)
    nxt = jnp.where(lane == N2 - 1, jnp.where(row == r1 - 1, 0.0, pltpu.roll(a, r1 - 1, 0)), a)
    return w0 * prev + w1 * x + w2 * nxt + b


def _hyconv_body(taps_ref, skip_ref, u_ref, g_ref, kf_ref, f1a_ref, twr_ref, twi_ref, w2f_ref,
                 w2i_ref, f1i_ref, o_ref, sa_ref, sq_ref, sx_ref, *, cb, kp, conv_u, u_off,
                 g_off, order):
    j = pl.program_id(0)

    def taps(ch):
        return (taps_ref[ch * 4], taps_ref[ch * 4 + 1], taps_ref[ch * 4 + 2], taps_ref[ch * 4 + 3])

    def fwd(c, carry):
        x = u_ref[c]
        if conv_u:
            x = _short_conv(x, *taps(u_off + j * cb + c))
        sx_ref[c] = x
        sa_ref[pl.ds(pl.multiple_of(c * kp, 8), kp), :] = _fft_rows(x, f1a_ref, twr_ref, twi_ref, kp)
        return carry

    lax.fori_loop(0, cb, fwd, 0)
    x = jnp.dot(sa_ref[...].astype(BF16), w2f_ref[...], preferred_element_type=F32)
    k = kf_ref[0].reshape(cb * kp, 2 * N2)
    xr, xi, kr, ki = x[:, :N2], x[:, N2:], k[:, :N2], k[:, N2:]
    p = jnp.concatenate([xr * kr - xi * ki, xr * ki + xi * kr], axis=1).astype(BF16)
    sq_ref[...] = jnp.dot(p, w2i_ref[...], preferred_element_type=F32)

    def inv(c, carry):
        q = sq_ref[pl.ds(pl.multiple_of(c * kp, 8), kp), :]
        qr, qi = q[:, :N2], q[:, N2:]
        twr, twi = twr_ref[...], twi_ref[...]
        y2 = jnp.concatenate([qr * twr + qi * twi, qi * twr - qr * twi], axis=0).astype(BF16)
        y = jnp.dot(f1i_ref[...], y2, preferred_element_type=F32)
        x = sx_ref[c]
        y = y + skip_ref[order * HY_W + j * cb + c] * x
        o_ref[c] = _short_conv(g_ref[c], *taps(g_off + j * cb + c)) * y
        return carry

    lax.fori_loop(0, cb, inv, 0)


def _hyena_conv(taps, skip, u3, u_blk0, g3, g_blk0, kf, tb, cb, *, conv_u, u_off, g_off, order):
    r1, kp = tb["r1"], tb["kp"]
    smem = pl.BlockSpec(memory_space=pltpu.SMEM)
    return pl.pallas_call(
        functools.partial(_hyconv_body, cb=cb, kp=kp, conv_u=conv_u, u_off=u_off, g_off=g_off,
                          order=order),
        grid=(HY_W // cb,),
        in_specs=[smem, smem,
                  pl.BlockSpec((cb, r1, N2), lambda j: (j + u_blk0, 0, 0)),
                  pl.BlockSpec((cb, r1, N2), lambda j: (j + g_blk0, 0, 0)),
                  pl.BlockSpec((1, cb, kp, 2 * N2), lambda j: (order, j, 0, 0)),
                  _full((2 * kp, r1)), _full((kp, N2)), _full((kp, N2)),
                  _full((2 * N2, 2 * N2)), _full((2 * N2, 2 * N2)), _full((r1, 2 * kp))],
        out_specs=pl.BlockSpec((cb, r1, N2), lambda j: (j, 0, 0)),
        out_shape=jax.ShapeDtypeStruct((HY_W, r1, N2), F32),
        scratch_shapes=[pltpu.VMEM((cb * kp, 2 * N2), F32),
                        pltpu.VMEM((cb * kp, 2 * N2), F32),
                        pltpu.VMEM((cb, r1, N2), F32)],
        compiler_params=_cparams(("arbitrary",), V7X_VMEM_LIMIT),
        name="hyena_conv%d" % order,
    )(taps, skip, u3, g3, kf, tb["f1a"], tb["twr"], tb["twi"], tb["w2f"], tb["w2i"], tb["f1i"])


def _ctx_tables(n):
    big = 2 * n
    a = 2 * np.pi * np.outer(np.arange(n), np.arange(big)) / big
    fc = np.concatenate([np.cos(a), -np.sin(a)], axis=1)
    gi = np.concatenate([np.cos(a).T, -np.sin(a).T], axis=0) / big
    return jnp.asarray(fc, BF16), jnp.asarray(gi, BF16)


def _hyctx_body(hy_ref, filt_ref, taps_ref, skip_ref, fc_ref, gi_ref, o_ref, *, n):
    big = 2 * n
    lane = lax.broadcasted_iota(jnp.int32, (HY_W, n), 1)

    def sconv(x, t):
        prev = jnp.where(lane == 0, 0.0, pltpu.roll(x, 1, 1))
        nxt = jnp.where(lane == n - 1, 0.0, pltpu.roll(x, n - 1, 1))
        return t[:, 0:1] * prev + t[:, 1:2] * x + t[:, 2:3] * nxt + t[:, 3:4]

    def spec(o):
        xf = _dot(filt_ref[2 * o], fc_ref[...])
        xb = _dot(jnp.where(lane == 0, 0.0, filt_ref[2 * o + 1]), fc_ref[...])
        return xf[:, :big] + xb[:, :big], xf[:, big:] - xb[:, big:]

    def lconv(u, o):
        kr, ki = spec(o)
        x = _dot(u, fc_ref[...])
        xr, xi = x[:, :big], x[:, big:]
        p = jnp.concatenate([xr * kr - xi * ki, xr * ki + xi * kr], axis=1)
        return _dot(p, gi_ref[...]) + skip_ref[o * HY_W:(o + 1) * HY_W, :] * u

    v = sconv(hy_ref[0:HY_W, :], taps_ref[0:HY_W, :])
    x1 = sconv(hy_ref[HY_W:2 * HY_W, :], taps_ref[HY_W:2 * HY_W, :])
    x2 = sconv(hy_ref[2 * HY_W:, :], taps_ref[2 * HY_W:, :])
    z = x1 * lconv(v, 0)
    o_ref[...] = x2 * lconv(z, 1)


def _hyena_ctx(hyT, blk, filt_ctx, taps2d, skip2d):
    n = filt_ctx.shape[-1]
    fc, gi = _ctx_tables(n)
    return pl.pallas_call(
        functools.partial(_hyctx_body, n=n),
        grid=(1,),
        in_specs=[pl.BlockSpec((HY_COLS, n), lambda i: (0, blk)),
                  _full((4, HY_W, n)), _full((HY_COLS, 4)), _full((2 * HY_W, 1)),
                  _full((n, 4 * n)), _full((4 * n, n))],
        out_specs=_full((HY_W, n)),
        out_shape=jax.ShapeDtypeStruct((HY_W, n), F32),
        compiler_params=_cparams(("arbitrary",), V7X_VMEM_LIMIT),
        name="hyena_ctx",
    )(hyT, filt_ctx, taps2d, skip2d, fc, gi)


def _ssd_body(xf_ref, xfp_ref, xfn_ref, dtf_ref, xb_ref, xbp_ref, xbn_ref, dtb_ref,
              cw_ref, cb_ref, dtbias_ref, alog_ref, dexp_ref, ex_ref,
              yf_ref, yb_ref, h_ref, sx_ref, *, nlat):
    j = pl.program_id(0)
    nch = nlat + CTX // CHUNK

    @pl.when(j == 0)
    def _():
        h_ref[...] = jnp.zeros_like(h_ref)

    ri = lax.broadcasted_iota(jnp.int32, (CHUNK, CHUNK), 0)
    ci = lax.broadcasted_iota(jnp.int32, (CHUNK, CHUNK), 1)
    a_all = -jnp.exp(alog_ref[...])

    for d in range(2):
        x_ref, xp_ref, xn_ref, dt_ref, y_ref = (
            (xf_ref, xfp_ref, xfn_ref, dtf_ref, yf_ref) if d == 0
            else (xb_ref, xbp_ref, xbn_ref, dtb_ref, yb_ref))
        cid = jnp.where(j < 2, nlat + j, j - 2) if d == 0 else nch - 1 - j
        pv = ((cid != 0) & (cid != nlat)).astype(F32)
        nv = ((cid != nlat - 1) & (cid != nch - 1)).astype(F32)
        sx_ref[0:8, :] = xp_ref[...] * pv
        sx_ref[8:8 + CHUNK, :] = x_ref[...]
        sx_ref[8 + CHUNK:, :] = xn_ref[...] * nv
        pre = (cw_ref[0:1, :] * sx_ref[pl.ds(7, CHUNK), :] + cw_ref[1:2, :] * x_ref[...]
               + cw_ref[2:3, :] * sx_ref[pl.ds(9, CHUNK), :] + cb_ref[...])
        xc = _silu(pre)
        xs = xc[:, 0:SSD_W]
        z = dt_ref[...] + dtbias_ref[...]
        dtv = jnp.maximum(z, 0.0) + jnp.log1p(jnp.exp(-jnp.abs(z)))
        da = dtv * a_all
        tri = (ri >= ci) if d == 0 else (ri <= ci)
        acum = _dot_hi(tri.astype(F32), da)
        last = CHUNK - 1 if d == 0 else 0
        tot = acum[last:last + 1, :]
        acum_t = acum.T
        ys = []
        for g in range(SSD_G):
            bm = xc[:, SSD_W + g * SSD_N:SSD_W + (g + 1) * SSD_N]
            cm = xc[:, SSD_W + (SSD_G + g) * SSD_N:SSD_W + (SSD_G + g + 1) * SSD_N]
            bt = bm.T
            scores = _dot(cm, bt)
            ex = ex_ref[d * SSD_G + g]
            acum_e = _dot_hi(acum, ex)
            dt_e = _dot_hi(dtv, ex)
            tot_e = _dot_hi(jnp.broadcast_to(tot, (8, LANES)), ex)[0:1]
            xg = xs[:, g * SSD_R * SSD_P:(g + 1) * SSD_R * SSD_P]
            hg = h_ref[d, g]
            yoff = _dot(cm, hg) * jnp.exp(acum_e)
            ydiag = []
            for r in range(SSD_R):
                hl = d * SSD_H + g * SSD_R + r
                seg = acum[:, hl:hl + 1] - acum_t[hl:hl + 1, :]
                m = scores * jnp.exp(jnp.where(tri, seg, -jnp.inf))
                xdt = xg[:, r * SSD_P:(r + 1) * SSD_P] * dt_e[:, r * SSD_P:(r + 1) * SSD_P]
                ydiag.append(_dot(m, xdt))
            ys.append(jnp.concatenate(ydiag, axis=1) + yoff)
            xw = xg * (jnp.exp(tot_e - acum_e) * dt_e)
            h_ref[d, g] = hg * jnp.exp(tot_e) + _dot(bt, xw)
        y = jnp.concatenate(ys, axis=1)
        if d == 0:
            y = y + dexp_ref[...] * xs
        y_ref[...] = y


def _ssd(xbc, dt, conv_w, conv_b, dt_bias, a_log, d_exp, ex, nlat):
    t = xbc.shape[0]
    nch = t // CHUNK
    hb = CHUNK // 8
    last8 = t // 8 - 1
    fc = lambda j: jnp.where(j < 2, nlat + j, j - 2)
    bc = lambda j: nch - 1 - j
    prev = lambda f: (lambda j: (jnp.maximum(f(j) * hb - 1, 0), 0))
    nxt = lambda f: (lambda j: (jnp.minimum((f(j) + 1) * hb, last8), 0))
    main = lambda f: (lambda j: (f(j), 0))
    return pl.pallas_call(
        functools.partial(_ssd_body, nlat=nlat),
        grid=(nch,),
        in_specs=[pl.BlockSpec((CHUNK, SSD_XBC), main(fc)), pl.BlockSpec((8, SSD_XBC), prev(fc)),
                  pl.BlockSpec((8, SSD_XBC), nxt(fc)), pl.BlockSpec((CHUNK, LANES), main(fc)),
                  pl.BlockSpec((CHUNK, SSD_XBC), main(bc)), pl.BlockSpec((8, SSD_XBC), prev(bc)),
                  pl.BlockSpec((8, SSD_XBC), nxt(bc)), pl.BlockSpec((CHUNK, LANES), main(bc)),
                  _full((3, SSD_XBC)), _full((1, SSD_XBC)), _full((1, LANES)), _full((1, LANES)),
                  _full((1, SSD_W)), _full((2 * SSD_G, LANES, SSD_R * SSD_P))],
        out_specs=[pl.BlockSpec((CHUNK, SSD_W), main(fc)), pl.BlockSpec((CHUNK, SSD_W), main(bc))],
        out_shape=[jax.ShapeDtypeStruct((t, SSD_W), F32), jax.ShapeDtypeStruct((t, SSD_W), F32)],
        scratch_shapes=[pltpu.VMEM((2, SSD_G, SSD_N, SSD_R * SSD_P), F32),
                        pltpu.VMEM((CHUNK + 16, SSD_XBC), F32)],
        compiler_params=_cparams(("arbitrary",), V7X_VMEM_LIMIT),
        name="ssd_scan",
    )(xbc, xbc, xbc, dt, xbc, xbc, xbc, dt, conv_w, conv_b, dt_bias, a_log, d_exp, ex)


def _attn_prep_body(qkv_ref, cos_ref, sin_ref, qn_ref, kn_ref, gsum_ref, q_ref, k_ref, v_ref):
    x = qkv_ref[...]
    qk = x[:, 0:AT_W + AT_KVW]
    ms = _dot_hi(qk * qk, gsum_ref[...])
    wn = jnp.concatenate([qn_ref[...], kn_ref[...]], axis=1)
    qk = qk * lax.rsqrt(ms + EPS) * wn
    w = qk.shape[1]
    lane = lax.broadcasted_iota(jnp.int32, qk.shape, 1)
    partner = jnp.where(lane % AT_D < AT_D // 2, pltpu.roll(qk, w - AT_D // 2, 1),
                        pltpu.roll(qk, AT_D // 2, 1))
    cos = jnp.concatenate([cos_ref[...]] * 3, axis=1)
    sin = jnp.concatenate([sin_ref[...]] * 3, axis=1)
    qk = qk * cos + partner * sin
    q = qk[:, 0:AT_W] * (AT_D ** -0.5)
    zeros = jnp.zeros((q.shape[0], AT_D), F32)
    for h in range(AT_H):
        qh = q[:, h * AT_D:(h + 1) * AT_D]
        q_ref[h] = (jnp.concatenate([qh, zeros], axis=1) if h // 2 == 0
                    else jnp.concatenate([zeros, qh], axis=1)).astype(BF16)
    k_ref[...] = qk[:, AT_W:].astype(BF16)
    v_ref[...] = x[:, AT_W + AT_KVW:].astype(BF16)


def _attn_prep(qkv, cos_t, sin_t, qn, kn, gsum):
    t = qkv.shape[0]
    row = lambda i: (i, 0)
    return pl.pallas_call(
        _attn_prep_body,
        grid=(t // TM,),
        in_specs=[pl.BlockSpec((TM, AT_W + 2 * AT_KVW), row), pl.BlockSpec((TM, LANES), row),
                  pl.BlockSpec((TM, LANES), row), _full((1, AT_W)), _full((1, AT_KVW)),
                  _full((AT_W + AT_KVW, AT_W + AT_KVW))],
        out_specs=[pl.BlockSpec((AT_H, TM, LANES), lambda i: (0, i, 0)),
                   pl.BlockSpec((TM, LANES), row), pl.BlockSpec((TM, LANES), row)],
        out_shape=[jax.ShapeDtypeStruct((AT_H, t, LANES), BF16),
                   jax.ShapeDtypeStruct((t, LANES), BF16), jax.ShapeDtypeStruct((t, LANES), BF16)],
        compiler_params=_cparams(("arbitrary",)),
        name="attn_prep",
    )(qkv, cos_t, sin_t, qn, kn, gsum)


def _assemble_heads(accs, tq):
    lane = lax.broadcasted_iota(jnp.int32, (tq, LANES), 1)
    outs = []
    for g in range(AT_KV):
        a0, a1 = accs[g][:tq], accs[g][tq:]
        if g == 0:
            outs.append(jnp.where(lane < AT_D, a0, pltpu.roll(a1, AT_D, 1)))
        else:
            outs.append(jnp.where(lane < AT_D, pltpu.roll(a0, AT_D, 1), a1))
    return jnp.concatenate(outs, axis=1)


def _flash_body(q_ref, k_ref, v_ref, o_ref, *, tq, kc, nkc):
    accs = []
    for g in range(AT_KV):
        q = jnp.concatenate([q_ref[2 * g], q_ref[2 * g + 1]], axis=0)

        def step(c, carry, q=q):
            m, l, acc = carry
            k = k_ref[pl.ds(pl.multiple_of(c * kc, kc), kc), :]
            v = v_ref[pl.ds(pl.multiple_of(c * kc, kc), kc), :]
            s = lax.dot_general(q, k, (((1,), (1,)), ((), ())), preferred_element_type=F32)
            m_new = jnp.maximum(m, jnp.max(s, axis=1, keepdims=True))
            p = jnp.exp(s - m_new)
            alpha = jnp.exp(m - m_new)
            l = alpha * l + jnp.sum(p, axis=1, keepdims=True)
            acc = alpha * acc + jnp.dot(p.astype(BF16), v, preferred_element_type=F32)
            return m_new, l, acc

        init = (jnp.full((2 * tq, 1), -jnp.inf, F32), jnp.zeros((2 * tq, 1), F32),
                jnp.zeros((2 * tq, LANES), F32))
        m, l, acc = lax.fori_loop(0, nkc, step, init)
        accs.append(acc / l)
    o_ref[...] = _assemble_heads(accs, tq)


def _flash_attention(qz, k, v, nq_rows, tq, kv_rows, q_blk0=0):
    nd = kv_rows // LANES
    div = max(d for d in range(1, 11) if nd % d == 0)
    kc = div * LANES
    return pl.pallas_call(
        functools.partial(_flash_body, tq=tq, kc=kc, nkc=kv_rows // kc),
        grid=(nq_rows // tq,),
        in_specs=[pl.BlockSpec((AT_H, tq, LANES), lambda i: (0, i + q_blk0, 0)),
                  pl.BlockSpec((kv_rows, LANES), lambda i: (0, 0)),
                  pl.BlockSpec((kv_rows, LANES), lambda i: (0, 0))],
        out_specs=pl.BlockSpec((tq, AT_W), lambda i: (i, 0)),
        out_shape=jax.ShapeDtypeStruct((nq_rows, AT_W), F32),
        compiler_params=_cparams(("arbitrary",), V7X_VMEM_LIMIT),
        name="flash_attention",
    )(qz, k, v)


def _layernorm(x, g, b):
    mu = jnp.mean(x, axis=-1, keepdims=True)
    xc = x - mu
    var = jnp.mean(xc * xc, axis=-1, keepdims=True)
    return xc * lax.rsqrt(var + EPS) * g + b


def _outproj_body(*refs, with_ctx, nlat_tiles, alpha):
    if with_ctx:
        (hyl_ref, hyc_ref, yf_ref, yb_ref, z_ref, atl_ref, atc_ref, x_ref, g1_ref, sc2_ref, sh2_ref,
         w_ref, nw_ref, lg_ref, lb_ref, x1_ref, h2_ref) = refs
    else:
        (hyl_ref, yf_ref, yb_ref, z_ref, atl_ref, x_ref, g1_ref, sc2_ref, sh2_ref,
         w_ref, nw_ref, lg_ref, lb_ref, x1_ref, h2_ref) = refs
    hy, at = hyl_ref[...], atl_ref[...]
    if with_ctx:
        is_ctx = pl.program_id(0) == nlat_tiles
        hy = jnp.where(is_ctx, hyc_ref[...], hy)
        at = jnp.where(is_ctx, atc_ref[...], at)
    m = _dot(hy.T, w_ref[0:HY_W, :])
    gated = (yf_ref[...] + yb_ref[...]) * _silu(z_ref[...])
    gw = SSD_W // SSD_G
    parts = []
    for g in range(SSD_G):
        gg = gated[:, g * gw:(g + 1) * gw]
        parts.append(gg * lax.rsqrt(jnp.mean(gg * gg, axis=-1, keepdims=True) + EPS))
    ssd = jnp.concatenate(parts, axis=1) * nw_ref[...]
    m = m + _dot(ssd, w_ref[HY_W:HY_W + SSD_W, :]) + _dot(at, w_ref[HY_W + SSD_W:, :])
    x1 = _layernorm(alpha * x_ref[...] + g1_ref[0] * m, lg_ref[...], lb_ref[...])
    x1_ref[...] = x1
    h2_ref[...] = (x1 * (1.0 + sc2_ref[0]) + sh2_ref[0]).astype(BF16)


def _out_proj(hy_lat, hy_ctx, yf, yb, z, at_lat, at_ctx, x_tok, g1, sc2, sh2, w_out, nw, lg, lb,
              *, with_ctx, alpha):
    t = x_tok.shape[0]
    nlat_tiles = hy_lat.shape[1] // TM
    nt = t // TM if with_ctx else nlat_tiles
    last = t // TM - 1
    sel = lambda i: (jnp.where(i == last, 1, 0), 0, 0)
    row = lambda i: (i, 0)
    lat_row = lambda i: (jnp.minimum(i, nlat_tiles - 1), 0)
    lat_col = lambda i: (0, jnp.minimum(i, nlat_tiles - 1))
    vec = pl.BlockSpec((1, 1, D_MODEL), sel)
    specs = [pl.BlockSpec((HY_W, TM), lat_col)]
    args = [hy_lat]
    if with_ctx:
        specs.append(_full((HY_W, TM)))
        args.append(hy_ctx)
    specs += [pl.BlockSpec((TM, SSD_W), row)] * 3 + [pl.BlockSpec((TM, AT_W), lat_row)]
    args += [yf, yb, z, at_lat]
    if with_ctx:
        specs.append(_full((TM, AT_W)))
        args.append(at_ctx)
    specs += [pl.BlockSpec((TM, D_MODEL), row), vec, vec, vec, _full((MIX_W, D_MODEL)),
              _full((1, SSD_W)), _full((1, D_MODEL)), _full((1, D_MODEL))]
    args += [x_tok, g1, sc2, sh2, w_out, nw, lg, lb]
    return pl.pallas_call(
        functools.partial(_outproj_body, with_ctx=with_ctx, nlat_tiles=nlat_tiles, alpha=alpha),
        grid=(nt,),
        in_specs=specs,
        out_specs=[pl.BlockSpec((TM, D_MODEL), row), pl.BlockSpec((TM, D_MODEL), row)],
        out_shape=[jax.ShapeDtypeStruct((nt * TM, D_MODEL), F32),
                   jax.ShapeDtypeStruct((nt * TM, D_MODEL), BF16)],
        compiler_params=_cparams(("arbitrary",), V7X_VMEM_LIMIT),
        name="out_proj",
    )(*args)


HALO = 16


def _ffn_body(h_ref, hp_ref, hn_ref, x1_ref, g2_ref, wup_ref, cw_ref, cb_ref, wdn_ref, lg_ref, lb_ref,
              o_ref, su_ref, *, nlat_tiles, ntiles, alpha):
    i = pl.program_id(0)
    pv = ((i != 0) & (i != nlat_tiles)).astype(F32)
    nv = ((i != nlat_tiles - 1) & (i != ntiles - 1)).astype(F32)
    hext = jnp.concatenate([hp_ref[...], h_ref[...], hn_ref[...]], axis=0)
    rows = lax.broadcasted_iota(jnp.int32, (TM + 2 * HALO, 1), 0)
    keep = jnp.where(rows < HALO, pv, jnp.where(rows >= TM + HALO, nv, 1.0))
    acc = jnp.zeros((TM, D_MODEL), F32)
    for c in range(D_FF // FF_CHUNK):
        c0 = c * FF_CHUNK
        su_ref[:, 0:FF_CHUNK] = jnp.dot(hext, wup_ref[:, c0:c0 + FF_CHUNK],
                                        preferred_element_type=F32) * keep
        su_ref[:, FF_CHUNK:] = jnp.dot(hext, wup_ref[:, D_FF + c0:D_FF + c0 + FF_CHUNK],
                                       preferred_element_type=F32) * keep
        cw = jnp.concatenate([cw_ref[:, c0:c0 + FF_CHUNK], cw_ref[:, D_FF + c0:D_FF + c0 + FF_CHUNK]],
                             axis=1)
        cb = jnp.concatenate([cb_ref[:, c0:c0 + FF_CHUNK], cb_ref[:, D_FF + c0:D_FF + c0 + FF_CHUNK]],
                             axis=1)
        u = (cw[0:1] * su_ref[pl.ds(HALO - 1, TM), :] + cw[1:2] * su_ref[pl.ds(HALO, TM), :]
             + cw[2:3] * su_ref[pl.ds(HALO + 1, TM), :] + cb)
        act = (_silu(u[:, 0:FF_CHUNK]) * u[:, FF_CHUNK:]).astype(BF16)
        acc = acc + jnp.dot(act, wdn_ref[c0:c0 + FF_CHUNK, :], preferred_element_type=F32)
    o_ref[...] = _layernorm(alpha * x1_ref[...] + g2_ref[0] * acc, lg_ref[...], lb_ref[...])


def _conv_ffn(h2, x1, g2, w_up, conv_w, conv_b, w_down, lg, lb, *, nlat_tiles, ntiles, alpha):
    hb = TM // HALO
    lastb = ntiles * hb - 1
    row = lambda i: (i, 0)
    sel = lambda i: (jnp.where(i == nlat_tiles, 1, 0), 0, 0)
    return pl.pallas_call(
        functools.partial(_ffn_body, nlat_tiles=nlat_tiles, ntiles=ntiles, alpha=alpha),
        grid=(ntiles,),
        in_specs=[pl.BlockSpec((TM, D_MODEL), row),
                  pl.BlockSpec((HALO, D_MODEL), lambda i: (jnp.maximum(i * hb - 1, 0), 0)),
                  pl.BlockSpec((HALO, D_MODEL), lambda i: (jnp.minimum((i + 1) * hb, lastb), 0)),
                  pl.BlockSpec((TM, D_MODEL), row),
                  pl.BlockSpec((1, 1, D_MODEL), sel),
                  _full((D_MODEL, 2 * D_FF)), _full((3, 2 * D_FF)), _full((1, 2 * D_FF)),
                  _full((D_FF, D_MODEL)), _full((1, D_MODEL)), _full((1, D_MODEL))],
        out_specs=pl.BlockSpec((TM, D_MODEL), row),
        out_shape=jax.ShapeDtypeStruct((ntiles * TM, D_MODEL), F32),
        scratch_shapes=[pltpu.VMEM((TM + 2 * HALO, 2 * FF_CHUNK), F32)],
        compiler_params=_cparams(("arbitrary",), V7X_VMEM_LIMIT),
        name="conv_ffn",
    )(h2, h2, h2, x1, g2, w_up, conv_w, conv_b, w_down, lg, lb)


def _rope_tables(nlat):
    rows = nlat // GRID_W
    row = jnp.repeat(jnp.arange(rows, dtype=F32), GRID_W)
    col = jnp.tile(jnp.arange(GRID_W, dtype=F32), rows)
    nf = AT_D // 4
    inv = ROPE_BASE ** (-jnp.arange(nf, dtype=F32) / nf)
    ang = jnp.concatenate([row[:, None] * inv, col[:, None] * inv], axis=-1)
    cos, sin = jnp.cos(ang), jnp.sin(ang)
    cos_h = jnp.concatenate([cos, cos], axis=1)
    sin_h = jnp.concatenate([-sin, sin], axis=1)
    cos_t = jnp.concatenate([jnp.tile(cos_h, (1, LANES // AT_D)), jnp.ones((CTX, LANES), F32)], axis=0)
    sin_t = jnp.concatenate([jnp.tile(sin_h, (1, LANES // AT_D)), jnp.zeros((CTX, LANES), F32)], axis=0)
    return cos_t, sin_t


def _expansion():
    ex = np.zeros((2 * SSD_G, LANES, SSD_R * SSD_P), np.float32)
    for d in range(2):
        for g in range(SSD_G):
            for r in range(SSD_R):
                ex[d * SSD_G + g, d * SSD_H + g * SSD_R + r, r * SSD_P:(r + 1) * SSD_P] = 1.0
    return jnp.asarray(ex)


def _group_mean_matrix():
    w = AT_W + AT_KVW
    idx = np.arange(w) // AT_D
    return jnp.asarray((idx[:, None] == idx[None, :]).astype(np.float32) / AT_D)


def _pad_lanes(v, width=LANES):
    return jnp.pad(v, ((0, 0), (0, width - v.shape[1])))


def kernel(x, c, ctx, c_ctx, w_mod, b_mod, w_in, hy_conv_w, hy_conv_b, hy_ffn_w1, hy_ffn_b1, hy_ffn_w2, hy_ffn_b2, hy_ffn_w3, hy_freq, hy_bias, ssd_conv_w, ssd_conv_b, ssd_dt_bias, ssd_a_log, ssd_d, ssd_norm_w, attn_q_norm, attn_k_norm, w_out, ln1_g, ln1_b, ffn_w_up, ffn_conv_w, ffn_conv_b, ffn_w_down, ln2_g, ln2_b):
    nlat = x.shape[1]
    assert x.shape[0] == 1 and ctx.shape[1] == CTX and nlat % (N2 * 8) == 0
    t = nlat + CTX
    nlt = nlat // TM
    alpha = (2.0 * DEPTH) ** 0.25
    cb = 16

    x_tok = jnp.concatenate([x[0], ctx[0]], axis=0)
    cvec = jnp.zeros((8, D_MODEL), F32).at[0].set(c[0]).at[1].set(c_ctx)
    mod = _modulation(cvec, w_mod, b_mod)
    cos_t, sin_t = _rope_tables(nlat)
    ex = _expansion()
    gsum = _group_mean_matrix()
    tb = _fft_tables(nlat)

    for i in range(DEPTH):
        ctx_out = i < DEPTH - 1
        mv = mod[i, 0:2].reshape(2, 6, 1, D_MODEL)
        sh1, sc1, g1, sh2, sc2, g2 = (mv[:, q] for q in range(6))

        wi = w_in[i]
        o = HY_COLS
        w_z = wi[:, o:o + SSD_W]
        w_xbc = wi[:, o + SSD_W:o + SSD_W + SSD_XBC]
        w_dt = _pad_lanes(wi[:, o + SSD_W + SSD_XBC:o + SSD_W + SSD_XBC + 2 * SSD_H])
        w_at = wi[:, o + SSD_W + SSD_XBC + 2 * SSD_H:]
        w_main = jnp.concatenate([w_z, w_xbc, w_dt, w_at], axis=1).astype(BF16)
        w_hyT = wi[:, 0:HY_COLS].T.astype(BF16)

        hyT, z, xbc, dt, qkv = _in_proj(x_tok, sc1, sh1, w_main, w_hyT)

        w1t = _pad_lanes(hy_ffn_w1[i].T)
        fargs = (w1t, hy_ffn_b1[i][:, None], hy_ffn_w2[i].T, hy_ffn_b2[i][:, None],
                 hy_ffn_w3[i].T, hy_freq[i][:, None])
        taps2d = jnp.concatenate([hy_conv_w[i].T, hy_conv_b[i][:, None]], axis=1)
        taps = taps2d.reshape(-1)
        skip = hy_bias[i].reshape(-1)
        filt = _hyena_filters(nlat, *fargs).reshape(4, HY_W, tb["r1"], N2)
        kf = _filter_spectra(filt, tb, cb)
        hy3 = hyT.reshape(HY_COLS, t // N2, N2)
        zz = _hyena_conv(taps, skip, hy3, 0, hy3, HY_W // cb, kf, tb, cb,
                         conv_u=True, u_off=0, g_off=HY_W, order=0)
        y_hy = _hyena_conv(taps, skip, zz, 0, hy3, 2 * HY_W // cb, kf, tb, cb,
                           conv_u=False, u_off=0, g_off=2 * HY_W, order=1)
        y_hy = y_hy.reshape(HY_W, nlat)
        if ctx_out:
            filt_c = _hyena_filters(CTX, *fargs)
            y_hy_c = _hyena_ctx(hyT, nlat // CTX, filt_c, taps2d, skip[:, None])
        else:
            y_hy_c = None

        d_exp = jnp.repeat(ssd_d[i], SSD_P)[None, :]
        yf, yb = _ssd(xbc, dt, ssd_conv_w[i], ssd_conv_b[i][None, :],
                      _pad_lanes(ssd_dt_bias[i].reshape(1, -1)), _pad_lanes(ssd_a_log[i].reshape(1, -1)),
                      d_exp, ex, nlat // CHUNK)

        qz, kk, vv = _attn_prep(qkv, cos_t, sin_t, jnp.tile(attn_q_norm[i], AT_H)[None, :],
                                jnp.tile(attn_k_norm[i], AT_KV)[None, :], gsum)
        y_at = _flash_attention(qz, kk, vv, nlat, 128, t)
        if ctx_out:
            y_at_c = _flash_attention(qz[:, nlat:], kk[nlat:], vv[nlat:], CTX, 128, CTX)
        else:
            y_at_c = None

        x1, h2 = _out_proj(y_hy, y_hy_c, yf, yb, z, y_at, y_at_c, x_tok, g1, sc2, sh2,
                           w_out[i].astype(BF16), ssd_norm_w[i][None, :], ln1_g[i][None, :],
                           ln1_b[i][None, :], with_ctx=ctx_out, alpha=alpha)
        ntiles = nlt + 1 if ctx_out else nlt
        x_tok = _conv_ffn(h2, x1, g2, ffn_w_up[i].astype(BF16), ffn_conv_w[i], ffn_conv_b[i][None, :],
                          ffn_w_down[i].astype(BF16), ln2_g[i][None, :], ln2_b[i][None, :],
                          nlat_tiles=nlt, ntiles=ntiles, alpha=alpha)
    return x_tok[None]
```

```python
import functools
import math

import numpy as np
import jax
import jax.numpy as jnp
from jax import lax
from jax.experimental import pallas as pl
from jax.experimental.pallas import tpu as pltpu

F32 = jnp.float32
BF16 = jnp.bfloat16
HI = lax.Precision.HIGHEST

D_MODEL = 1024
DEPTH = 2
GRID_W = 64
CTX = 256
EPS = 1e-6

HY_W = 256
HY_BANDS = 16
HY_EMB = 1 + 2 * HY_BANDS
HY_HID = 64
HY_MIN_DECAY = math.log(1e-2) / 1.5
HY_MAX_DECAY = math.log(1e-2) / 0.3
HY_COLS = 3 * HY_W

SSD_H = 8
SSD_P = 64
SSD_W = SSD_H * SSD_P
SSD_G = 2
SSD_R = SSD_H // SSD_G
SSD_N = 128
SSD_XBC = SSD_W + 2 * SSD_G * SSD_N
CHUNK = 128

AT_H = 4
AT_KV = 2
AT_D = 64
AT_W = AT_H * AT_D
AT_KVW = AT_KV * AT_D
ROPE_BASE = 10000.0

MIX_W = HY_W + SSD_W + AT_W
D_FF = 2816
FF_CHUNK = 256

TM = CTX
LANES = 128
N2 = 256
V7X_VMEM_LIMIT = 56 * 1024 * 1024


def _cparams(sem, vmem=None):
    return pltpu.CompilerParams(dimension_semantics=sem, vmem_limit_bytes=vmem)


def _dot(a, b):
    return jnp.dot(a.astype(BF16), b.astype(BF16), preferred_element_type=F32)


def _dot_hi(a, b):
    return jnp.dot(a, b, preferred_element_type=F32, precision=HI)


def _split(x, n):
    pieces = []
    for _ in range(n):
        p = x.astype(BF16)
        pieces.append(p)
        x = x - p.astype(F32)
    return pieces


def _sigmoid(x):
    return 1.0 / (1.0 + jnp.exp(-x))


def _silu(x):
    return x * _sigmoid(x)


def _full(shape):
    nd = len(shape)
    return pl.BlockSpec(shape, lambda *_: (0,) * nd)


def _mod_body(c_ref, w_ref, b_ref, o_ref):
    s = _silu(c_ref[...])
    o_ref[0] = _dot(s, w_ref[0]) + b_ref[0]


def _modulation(cvec, w_mod, b_mod):
    depth, d, n = w_mod.shape
    nb = 1536
    return pl.pallas_call(
        _mod_body,
        grid=(depth, n // nb),
        in_specs=[pl.BlockSpec((8, d), lambda l, j: (0, 0)),
                  pl.BlockSpec((1, d, nb), lambda l, j: (l, 0, j)),
                  pl.BlockSpec((1, 1, nb), lambda l, j: (l, 0, j))],
        out_specs=pl.BlockSpec((1, 8, nb), lambda l, j: (l, 0, j)),
        out_shape=jax.ShapeDtypeStruct((depth, 8, n), F32),
        compiler_params=_cparams(("arbitrary", "arbitrary")),
        name="modulation",
    )(cvec, w_mod, b_mod.reshape(depth, 1, n))


W_MAIN = SSD_W + SSD_XBC + LANES + AT_W + 2 * AT_KVW


def _inproj_body(x_ref, sc_ref, sh_ref, wm_ref, wh_ref, hy_ref, z_ref, xbc_ref, dt_ref, qkv_ref):
    h = (x_ref[...] * (1.0 + sc_ref[0]) + sh_ref[0]).astype(BF16)
    main = jnp.dot(h, wm_ref[...], preferred_element_type=F32)
    z_ref[...] = main[:, 0:SSD_W]
    xbc_ref[...] = main[:, SSD_W:SSD_W + SSD_XBC]
    o = SSD_W + SSD_XBC
    dt_ref[...] = main[:, o:o + LANES]
    qkv_ref[...] = main[:, o + LANES:]
    hy_ref[...] = lax.dot_general(wh_ref[...], h, (((1,), (1,)), ((), ())),
                                  preferred_element_type=F32)


def _in_proj(x_tok, sc, sh, w_main, w_hyT):
    t = x_tok.shape[0]
    nt = t // TM
    sel = lambda i: (jnp.where(i == nt - 1, 1, 0), 0, 0)
    row = lambda i: (i, 0)
    return pl.pallas_call(
        _inproj_body,
        grid=(nt,),
        in_specs=[pl.BlockSpec((TM, D_MODEL), row),
                  pl.BlockSpec((1, 1, D_MODEL), sel),
                  pl.BlockSpec((1, 1, D_MODEL), sel),
                  _full((D_MODEL, W_MAIN)),
                  _full((HY_COLS, D_MODEL))],
        out_specs=[pl.BlockSpec((HY_COLS, TM), lambda i: (0, i)),
                   pl.BlockSpec((TM, SSD_W), row),
                   pl.BlockSpec((TM, SSD_XBC), row),
                   pl.BlockSpec((TM, LANES), row),
                   pl.BlockSpec((TM, AT_W + 2 * AT_KVW), row)],
        out_shape=[jax.ShapeDtypeStruct((HY_COLS, t), F32),
                   jax.ShapeDtypeStruct((t, SSD_W), F32),
                   jax.ShapeDtypeStruct((t, SSD_XBC), F32),
                   jax.ShapeDtypeStruct((t, LANES), F32),
                   jax.ShapeDtypeStruct((t, AT_W + 2 * AT_KVW), F32)],
        compiler_params=_cparams(("arbitrary",), V7X_VMEM_LIMIT),
        name="in_proj",
    )(x_tok, sc, sh, w_main, w_hyT)


def _filter_body(w1_ref, b1_ref, w2_ref, b2_ref, w3_ref, fr_ref, o_ref, *, n, nt):
    j = pl.program_id(0)
    k = (lax.broadcasted_iota(jnp.int32, (1, nt), 1) + j * nt).astype(F32)
    t = k * (1.0 / (n - 1))
    wk = k * (2.0 * math.pi / n)
    r = lax.broadcasted_iota(jnp.int32, (LANES, nt), 0)
    band = jnp.where(r <= HY_BANDS, r - 1, r - 1 - HY_BANDS).astype(F32)
    f = 1e-4 + band * ((HY_BANDS - 1 - 1e-4) / (HY_BANDS - 1))
    arg = f * wk
    feats = jnp.where(r == 0, t, jnp.where(r <= HY_BANDS, jnp.cos(arg),
                                           jnp.where(r < HY_EMB, -jnp.sin(arg), 0.0)))
    fr = fr_ref[...]
    h = jnp.sin(fr * (_dot_hi(w1_ref[...], feats) + b1_ref[...]))
    h = jnp.sin(fr * (_dot_hi(w2_ref[...], h) + b2_ref[...]))
    h = _dot(w3_ref[...], h)
    c = lax.broadcasted_iota(jnp.int32, (HY_W, nt), 0).astype(F32)
    delta = jnp.abs(HY_MIN_DECAY + c * ((HY_MAX_DECAY - HY_MIN_DECAY) / (HY_W - 1)))
    window = jnp.exp(-t * delta)
    for q in range(4):
        o_ref[q] = h[q * HY_W:(q + 1) * HY_W] * window


def _hyena_filters(n, w1t, b1, w2t, b2, w3t, freq):
    nt = min(n, 2048)
    return pl.pallas_call(
        functools.partial(_filter_body, n=n, nt=nt),
        grid=(n // nt,),
        in_specs=[_full((HY_HID, LANES)), _full((HY_HID, 1)), _full((HY_HID, HY_HID)),
                  _full((HY_HID, 1)), _full((4 * HY_W, HY_HID)), _full((HY_HID, 1))],
        out_specs=pl.BlockSpec((4, HY_W, nt), lambda j: (0, 0, j)),
        out_shape=jax.ShapeDtypeStruct((4, HY_W, n), F32),
        compiler_params=_cparams(("arbitrary",)),
        name="hyena_filters",
    )(w1t, b1, w2t, b2, w3t, freq)


def _fft_tables(n):
    big = 2 * n
    n1 = big // N2
    r1 = n1 // 2
    kp = -(-(r1 + 1) // 8) * 8
    k1 = np.arange(kp)[:, None].astype(np.float64)
    live = (k1 <= r1)
    a1 = 2 * np.pi * k1 * np.arange(r1)[None, :] / n1
    f1a = np.concatenate([np.cos(a1) * live, -np.sin(a1) * live], axis=0)
    at = 2 * np.pi * k1 * np.arange(N2)[None, :] / big
    twr, twi = np.cos(at) * live, -np.sin(at) * live
    a2 = 2 * np.pi * np.outer(np.arange(N2), np.arange(N2)) / N2
    c2, s2 = np.cos(a2), np.sin(a2)
    w2f = np.block([[c2, -s2], [s2, c2]])
    w2i = np.block([[c2, s2], [-s2, c2]])
    wgt = np.where((k1 == 0) | (k1 == r1), 1.0, 2.0) * live / big
    f1i = np.concatenate([(np.cos(a1) * wgt).T, (-np.sin(a1) * wgt).T], axis=1)
    j = jnp.asarray
    return dict(r1=r1, kp=kp, f1a=j(f1a, BF16), twr=j(twr, F32), twi=j(twi, F32),
                w2f=j(w2f, BF16), w2i=j(w2i, BF16), f1i=j(f1i, BF16))


def _fft_rows(x, f1a_ref, twr_ref, twi_ref, kp):
    a = jnp.dot(f1a_ref[...], x.astype(BF16), preferred_element_type=F32)
    ar, ai = a[:kp], a[kp:]
    twr, twi = twr_ref[...], twi_ref[...]
    return jnp.concatenate([ar * twr - ai * twi, ar * twi + ai * twr], axis=1)


def _kspec_body(hf_ref, hb_ref, f1a_ref, twr_ref, twi_ref, w2f_ref, o_ref, sa_ref, *, cb, kp, r1):
    row = lax.broadcasted_iota(jnp.int32, (r1, N2), 0)
    lane = lax.broadcasted_iota(jnp.int32, (r1, N2), 1)
    first = (row == 0) & (lane == 0)

    def chan(c, carry):
        sa_ref[pl.ds(pl.multiple_of(c * kp, 8), kp), :] = _fft_rows(
            hf_ref[0, c], f1a_ref, twr_ref, twi_ref, kp)
        hb = jnp.where(first, 0.0, hb_ref[0, c])
        sa_ref[pl.ds(pl.multiple_of((cb + c) * kp, 8), kp), :] = _fft_rows(
            hb, f1a_ref, twr_ref, twi_ref, kp)
        return carry

    lax.fori_loop(0, cb, chan, 0)
    x = jnp.dot(sa_ref[...].astype(BF16), w2f_ref[...], preferred_element_type=F32)
    xf, xb = x[:cb * kp], x[cb * kp:]
    kf = jnp.concatenate([xf[:, :N2] + xb[:, :N2], xf[:, N2:] - xb[:, N2:]], axis=1)
    o_ref[0] = kf.reshape(cb, kp, 2 * N2)


def _filter_spectra(filt, tb, cb):
    r1, kp = tb["r1"], tb["kp"]
    return pl.pallas_call(
        functools.partial(_kspec_body, cb=cb, kp=kp, r1=r1),
        grid=(2, HY_W // cb),
        in_specs=[pl.BlockSpec((1, cb, r1, N2), lambda o, j: (2 * o, j, 0, 0)),
                  pl.BlockSpec((1, cb, r1, N2), lambda o, j: (2 * o + 1, j, 0, 0)),
                  _full((2 * kp, r1)), _full((kp, N2)), _full((kp, N2)), _full((2 * N2, 2 * N2))],
        out_specs=pl.BlockSpec((1, cb, kp, 2 * N2), lambda o, j: (o, j, 0, 0)),
        out_shape=jax.ShapeDtypeStruct((2, HY_W, kp, 2 * N2), F32),
        scratch_shapes=[pltpu.VMEM((2 * cb * kp, 2 * N2), F32)],
        compiler_params=_cparams(("arbitrary", "arbitrary"), V7X_VMEM_LIMIT),
        name="hyena_filter_spectra",
    )(filt, filt, tb["f1a"], tb["twr"], tb["twi"], tb["w2f"])


def _short_conv(x, w0, w1, w2, b):
    r1 = x.shape[0]
    row = lax.broadcasted_iota(jnp.int32, x.shape, 0)
    lane = lax.broadcasted_iota(jnp.int32, x.shape, 1)
    a = pltpu.roll(x, 1, 1)
    prev = jnp.where(lane == 0, jnp.where(row == 0, 0.0, pltpu.roll(a, 1, 0)), a)
    a = pltpu.roll(x, N2 - 1, 1)
    nxt = jnp.where(lane == N2 - 1, jnp.where(row == r1 - 1, 0.0, pltpu.roll(a, r1 - 1, 0)), a)
    return w0 * prev + w1 * x + w2 * nxt + b


def _hyconv_body(taps_ref, skip_ref, u_ref, g_ref, kf_ref, f1a_ref, twr_ref, twi_ref, w2f_ref,
                 w2i_ref, f1i_ref, o_ref, sa_ref, sq_ref, sx_ref, *, cb, kp, conv_u, u_off,
                 g_off, order):
    j = pl.program_id(0)

    def taps(ch):
        return (taps_ref[ch * 4], taps_ref[ch * 4 + 1], taps_ref[ch * 4 + 2], taps_ref[ch * 4 + 3])

    def fwd(c, carry):
        x = u_ref[c]
        if conv_u:
            x = _short_conv(x, *taps(u_off + j * cb + c))
        sx_ref[c] = x
        sa_ref[pl.ds(pl.multiple_of(c * kp, 8), kp), :] = _fft_rows(x, f1a_ref, twr_ref, twi_ref, kp)
        return carry

    lax.fori_loop(0, cb, fwd, 0)
    x = jnp.dot(sa_ref[...].astype(BF16), w2f_ref[...], preferred_element_type=F32)
    k = kf_ref[0].reshape(cb * kp, 2 * N2)
    xr, xi, kr, ki = x[:, :N2], x[:, N2:], k[:, :N2], k[:, N2:]
    p = jnp.concatenate([xr * kr - xi * ki, xr * ki + xi * kr], axis=1).astype(BF16)
    sq_ref[...] = jnp.dot(p, w2i_ref[...], preferred_element_type=F32)

    def inv(c, carry):
        q = sq_ref[pl.ds(pl.multiple_of(c * kp, 8), kp), :]
        qr, qi = q[:, :N2], q[:, N2:]
        twr, twi = twr_ref[...], twi_ref[...]
        y2 = jnp.concatenate([qr * twr + qi * twi, qi * twr - qr * twi], axis=0).astype(BF16)
        y = jnp.dot(f1i_ref[...], y2, preferred_element_type=F32)
        x = sx_ref[c]
        y = y + skip_ref[order * HY_W + j * cb + c] * x
        o_ref[c] = _short_conv(g_ref[c], *taps(g_off + j * cb + c)) * y
        return carry

    lax.fori_loop(0, cb, inv, 0)


def _hyena_conv(taps, skip, u3, u_blk0, g3, g_blk0, kf, tb, cb, *, conv_u, u_off, g_off, order):
    r1, kp = tb["r1"], tb["kp"]
    smem = pl.BlockSpec(memory_space=pltpu.SMEM)
    return pl.pallas_call(
        functools.partial(_hyconv_body, cb=cb, kp=kp, conv_u=conv_u, u_off=u_off, g_off=g_off,
                          order=order),
        grid=(HY_W // cb,),
        in_specs=[smem, smem,
                  pl.BlockSpec((cb, r1, N2), lambda j: (j + u_blk0, 0, 0)),
                  pl.BlockSpec((cb, r1, N2), lambda j: (j + g_blk0, 0, 0)),
                  pl.BlockSpec((1, cb, kp, 2 * N2), lambda j: (order, j, 0, 0)),
                  _full((2 * kp, r1)), _full((kp, N2)), _full((kp, N2)),
                  _full((2 * N2, 2 * N2)), _full((2 * N2, 2 * N2)), _full((r1, 2 * kp))],
        out_specs=pl.BlockSpec((cb, r1, N2), lambda j: (j, 0, 0)),
        out_shape=jax.ShapeDtypeStruct((HY_W, r1, N2), F32),
        scratch_shapes=[pltpu.VMEM((cb * kp, 2 * N2), F32),
                        pltpu.VMEM((cb * kp, 2 * N2), F32),
                        pltpu.VMEM((cb, r1, N2), F32)],
        compiler_params=_cparams(("arbitrary",), V7X_VMEM_LIMIT),
        name="hyena_conv%d" % order,
    )(taps, skip, u3, g3, kf, tb["f1a"], tb["twr"], tb["twi"], tb["w2f"], tb["w2i"], tb["f1i"])


def _ctx_tables(n):
    big = 2 * n
    a = 2 * np.pi * np.outer(np.arange(n), np.arange(big)) / big
    fc = np.concatenate([np.cos(a), -np.sin(a)], axis=1)
    gi = np.concatenate([np.cos(a).T, -np.sin(a).T], axis=0) / big
    return jnp.asarray(fc, BF16), jnp.asarray(gi, BF16)


def _hyctx_body(hy_ref, filt_ref, taps_ref, skip_ref, fc_ref, gi_ref, o_ref, *, n):
    big = 2 * n
    lane = lax.broadcasted_iota(jnp.int32, (HY_W, n), 1)

    def sconv(x, t):
        prev = jnp.where(lane == 0, 0.0, pltpu.roll(x, 1, 1))
        nxt = jnp.where(lane == n - 1, 0.0, pltpu.roll(x, n - 1, 1))
        return t[:, 0:1] * prev + t[:, 1:2] * x + t[:, 2:3] * nxt + t[:, 3:4]

    def spec(o):
        xf = _dot(filt_ref[2 * o], fc_ref[...])
        xb = _dot(jnp.where(lane == 0, 0.0, filt_ref[2 * o + 1]), fc_ref[...])
        return xf[:, :big] + xb[:, :big], xf[:, big:] - xb[:, big:]

    def lconv(u, o):
        kr, ki = spec(o)
        x = _dot(u, fc_ref[...])
        xr, xi = x[:, :big], x[:, big:]
        p = jnp.concatenate([xr * kr - xi * ki, xr * ki + xi * kr], axis=1)
        return _dot(p, gi_ref[...]) + skip_ref[o * HY_W:(o + 1) * HY_W, :] * u

    v = sconv(hy_ref[0:HY_W, :], taps_ref[0:HY_W, :])
    x1 = sconv(hy_ref[HY_W:2 * HY_W, :], taps_ref[HY_W:2 * HY_W, :])
    x2 = sconv(hy_ref[2 * HY_W:, :], taps_ref[2 * HY_W:, :])
    z = x1 * lconv(v, 0)
    o_ref[...] = x2 * lconv(z, 1)


def _hyena_ctx(hyT, blk, filt_ctx, taps2d, skip2d):
    n = filt_ctx.shape[-1]
    fc, gi = _ctx_tables(n)
    return pl.pallas_call(
        functools.partial(_hyctx_body, n=n),
        grid=(1,),
        in_specs=[pl.BlockSpec((HY_COLS, n), lambda i: (0, blk)),
                  _full((4, HY_W, n)), _full((HY_COLS, 4)), _full((2 * HY_W, 1)),
                  _full((n, 4 * n)), _full((4 * n, n))],
        out_specs=_full((HY_W, n)),
        out_shape=jax.ShapeDtypeStruct((HY_W, n), F32),
        compiler_params=_cparams(("arbitrary",), V7X_VMEM_LIMIT),
        name="hyena_ctx",
    )(hyT, filt_ctx, taps2d, skip2d, fc, gi)


def _ssd_body(xf_ref, xfp_ref, xfn_ref, dtf_ref, xb_ref, xbp_ref, xbn_ref, dtb_ref,
              cw_ref, cb_ref, dtbias_ref, alog_ref, dexp_ref, ex_ref,
              yf_ref, yb_ref, h_ref, sx_ref, *, nlat):
    j = pl.program_id(0)
    nch = nlat + CTX // CHUNK

    @pl.when(j == 0)
    def _():
        h_ref[...] = jnp.zeros_like(h_ref)

    ri = lax.broadcasted_iota(jnp.int32, (CHUNK, CHUNK), 0)
    ci = lax.broadcasted_iota(jnp.int32, (CHUNK, CHUNK), 1)
    a_all = -jnp.exp(alog_ref[...])

    for d in range(2):
        x_ref, xp_ref, xn_ref, dt_ref, y_ref = (
            (xf_ref, xfp_ref, xfn_ref, dtf_ref, yf_ref) if d == 0
            else (xb_ref, xbp_ref, xbn_ref, dtb_ref, yb_ref))
        cid = jnp.where(j < 2, nlat + j, j - 2) if d == 0 else nch - 1 - j
        pv = ((cid != 0) & (cid != nlat)).astype(F32)
        nv = ((cid != nlat - 1) & (cid != nch - 1)).astype(F32)
        sx_ref[0:8, :] = xp_ref[...] * pv
        sx_ref[8:8 + CHUNK, :] = x_ref[...]
        sx_ref[8 + CHUNK:, :] = xn_ref[...] * nv
        pre = (cw_ref[0:1, :] * sx_ref[pl.ds(7, CHUNK), :] + cw_ref[1:2, :] * x_ref[...]
               + cw_ref[2:3, :] * sx_ref[pl.ds(9, CHUNK), :] + cb_ref[...])
        xc = _silu(pre)
        xs = xc[:, 0:SSD_W]
        z = dt_ref[...] + dtbias_ref[...]
        dtv = jnp.maximum(z, 0.0) + jnp.log1p(jnp.exp(-jnp.abs(z)))
        da = dtv * a_all
        tri = (ri >= ci) if d == 0 else (ri <= ci)
        r3 = jnp.dot(tri.astype(BF16), jnp.concatenate(_split(da, 3), axis=1),
                     preferred_element_type=F32)
        acum = r3[:, 0:LANES] + r3[:, LANES:2 * LANES] + r3[:, 2 * LANES:]
        last = CHUNK - 1 if d == 0 else 0
        acum_t = acum.T
        pieces = jnp.concatenate(_split(acum, 2) + _split(dtv, 2), axis=1)
        ys = []
        for g in range(SSD_G):
            bm = xc[:, SSD_W + g * SSD_N:SSD_W + (g + 1) * SSD_N]
            cm = xc[:, SSD_W + (SSD_G + g) * SSD_N:SSD_W + (SSD_G + g + 1) * SSD_N]
            bt = bm.T
            scores = _dot(cm, bt)
            both = jnp.dot(pieces, ex_ref[d * SSD_G + g], preferred_element_type=F32)
            acum_e, dt_e = both[:, 0:SSD_R * SSD_P], both[:, SSD_R * SSD_P:]
            tot_e = acum_e[last:last + 1, :]
            xg = xs[:, g * SSD_R * SSD_P:(g + 1) * SSD_R * SSD_P]
            hg = h_ref[d, g]
            yoff = _dot(cm, hg) * jnp.exp(acum_e)
            ydiag = []
            for r in range(SSD_R):
                hl = d * SSD_H + g * SSD_R + r
                seg = acum[:, hl:hl + 1] - acum_t[hl:hl + 1, :]
                m = scores * jnp.exp(jnp.where(tri, seg, -jnp.inf))
                xdt = xg[:, r * SSD_P:(r + 1) * SSD_P] * dt_e[:, r * SSD_P:(r + 1) * SSD_P]
                ydiag.append(_dot(m, xdt))
            ys.append(jnp.concatenate(ydiag, axis=1) + yoff)
            xw = xg * (jnp.exp(tot_e - acum_e) * dt_e)
            h_ref[d, g] = hg * jnp.exp(tot_e) + _dot(bt, xw)
        y = jnp.concatenate(ys, axis=1)
        if d == 0:
            y = y + dexp_ref[...] * xs
        y_ref[...] = y


def _ssd(xbc, dt, conv_w, conv_b, dt_bias, a_log, d_exp, ex, nlat):
    t = xbc.shape[0]
    nch = t // CHUNK
    hb = CHUNK // 8
    last8 = t // 8 - 1
    fc = lambda j: jnp.where(j < 2, nlat + j, j - 2)
    bc = lambda j: nch - 1 - j
    prev = lambda f: (lambda j: (jnp.maximum(f(j) * hb - 1, 0), 0))
    nxt = lambda f: (lambda j: (jnp.minimum((f(j) + 1) * hb, last8), 0))
    main = lambda f: (lambda j: (f(j), 0))
    return pl.pallas_call(
        functools.partial(_ssd_body, nlat=nlat),
        grid=(nch,),
        in_specs=[pl.BlockSpec((CHUNK, SSD_XBC), main(fc)), pl.BlockSpec((8, SSD_XBC), prev(fc)),
                  pl.BlockSpec((8, SSD_XBC), nxt(fc)), pl.BlockSpec((CHUNK, LANES), main(fc)),
                  pl.BlockSpec((CHUNK, SSD_XBC), main(bc)), pl.BlockSpec((8, SSD_XBC), prev(bc)),
                  pl.BlockSpec((8, SSD_XBC), nxt(bc)), pl.BlockSpec((CHUNK, LANES), main(bc)),
                  _full((3, SSD_XBC)), _full((1, SSD_XBC)), _full((1, LANES)), _full((1, LANES)),
                  _full((1, SSD_W)), _full((2 * SSD_G, 4 * LANES, 2 * SSD_R * SSD_P))],
        out_specs=[pl.BlockSpec((CHUNK, SSD_W), main(fc)), pl.BlockSpec((CHUNK, SSD_W), main(bc))],
        out_shape=[jax.ShapeDtypeStruct((t, SSD_W), F32), jax.ShapeDtypeStruct((t, SSD_W), F32)],
        scratch_shapes=[pltpu.VMEM((2, SSD_G, SSD_N, SSD_R * SSD_P), F32),
                        pltpu.VMEM((CHUNK + 16, SSD_XBC), F32)],
        compiler_params=_cparams(("arbitrary",), V7X_VMEM_LIMIT),
        name="ssd_scan",
    )(xbc, xbc, xbc, dt, xbc, xbc, xbc, dt, conv_w, conv_b, dt_bias, a_log, d_exp, ex)


def _attn_prep_body(qkv_ref, cos_ref, sin_ref, qn_ref, kn_ref, gsum_ref, q_ref, k_ref, v0_ref, v1_ref):
    x = qkv_ref[...]
    qk = x[:, 0:AT_W + AT_KVW]
    ms = _dot_hi(qk * qk, gsum_ref[...])
    wn = jnp.concatenate([qn_ref[...], kn_ref[...]], axis=1)
    qk = qk * lax.rsqrt(ms + EPS) * wn
    w = qk.shape[1]
    lane = lax.broadcasted_iota(jnp.int32, qk.shape, 1)
    partner = jnp.where(lane % AT_D < AT_D // 2, pltpu.roll(qk, w - AT_D // 2, 1),
                        pltpu.roll(qk, AT_D // 2, 1))
    cos = jnp.concatenate([cos_ref[...]] * 3, axis=1)
    sin = jnp.concatenate([sin_ref[...]] * 3, axis=1)
    qk = qk * cos + partner * sin
    q = qk[:, 0:AT_W] * (AT_D ** -0.5 * math.log2(math.e))
    zeros = jnp.zeros((q.shape[0], AT_D), F32)
    for h in range(AT_H):
        qh = q[:, h * AT_D:(h + 1) * AT_D]
        q_ref[h] = (jnp.concatenate([qh, zeros], axis=1) if h // 2 == 0
                    else jnp.concatenate([zeros, qh], axis=1)).astype(BF16)
    k_ref[...] = qk[:, AT_W:].astype(BF16)
    v = x[:, AT_W + AT_KVW:]
    vl = lax.broadcasted_iota(jnp.int32, v.shape, 1)
    v0_ref[...] = jnp.where(vl < AT_D, v, jnp.where(vl == AT_D, 1.0, 0.0)).astype(BF16)
    v1_ref[...] = jnp.where(vl >= AT_D, v, jnp.where(vl == 0, 1.0, 0.0)).astype(BF16)


def _attn_prep(qkv, cos_t, sin_t, qn, kn, gsum):
    t = qkv.shape[0]
    row = lambda i: (i, 0)
    return pl.pallas_call(
        _attn_prep_body,
        grid=(t // TM,),
        in_specs=[pl.BlockSpec((TM, AT_W + 2 * AT_KVW), row), pl.BlockSpec((TM, LANES), row),
                  pl.BlockSpec((TM, LANES), row), _full((1, AT_W)), _full((1, AT_KVW)),
                  _full((AT_W + AT_KVW, AT_W + AT_KVW))],
        out_specs=[pl.BlockSpec((AT_H, TM, LANES), lambda i: (0, i, 0)),
                   pl.BlockSpec((TM, LANES), row), pl.BlockSpec((TM, LANES), row),
                   pl.BlockSpec((TM, LANES), row)],
        out_shape=[jax.ShapeDtypeStruct((AT_H, t, LANES), BF16),
                   jax.ShapeDtypeStruct((t, LANES), BF16), jax.ShapeDtypeStruct((t, LANES), BF16),
                   jax.ShapeDtypeStruct((t, LANES), BF16)],
        compiler_params=_cparams(("arbitrary",)),
        name="attn_prep",
    )(qkv, cos_t, sin_t, qn, kn, gsum)


def _assemble_heads(accs, tq):
    lane = lax.broadcasted_iota(jnp.int32, (tq, LANES), 1)
    outs = []
    for g in range(AT_KV):
        a0, a1 = accs[g][:tq], accs[g][tq:]
        if g == 0:
            outs.append(jnp.where(lane < AT_D, a0, pltpu.roll(a1, AT_D, 1)))
        else:
            outs.append(jnp.where(lane < AT_D, pltpu.roll(a0, AT_D, 1), a1))
    return jnp.concatenate(outs, axis=1)


def _flash_body(q_ref, k_ref, v0_ref, v1_ref, o_ref, s_ref, *, tq, kc, nkc):
    v_refs = (v0_ref, v1_ref)
    den_lane = (AT_D, 0)
    qs = [jnp.concatenate([q_ref[2 * g], q_ref[2 * g + 1]], axis=0) for g in range(AT_KV)]

    def rows_of(c):
        return pl.ds(pl.multiple_of(c * kc, kc), kc)

    def scores(c, slot):
        k = k_ref[rows_of(c), :]
        for g in range(AT_KV):
            s_ref[slot, g] = lax.dot_general(qs[g], k, (((1,), (1,)), ((), ())),
                                             preferred_element_type=F32)

    def update(c, slot, carry):
        out = []
        for g, (m, acc) in enumerate(carry):
            s = s_ref[slot, g]
            m_new = jnp.maximum(m, jnp.max(s, axis=1, keepdims=True))
            p = jnp.exp2(s - m_new).astype(BF16)
            acc = jnp.exp2(m - m_new) * acc + jnp.dot(p, v_refs[g][rows_of(c), :],
                                                      preferred_element_type=F32)
            out.append((m_new, acc))
        return tuple(out)

    def pair(i, carry):
        scores(2 * i + 1, 1)
        carry = update(2 * i, 0, carry)
        scores(2 * i + 2, 0)
        return update(2 * i + 1, 1, carry)

    scores(0, 0)
    carry = tuple((jnp.full((2 * tq, 1), -jnp.inf, F32), jnp.zeros((2 * tq, LANES), F32))
                  for _ in range(AT_KV))
    carry = lax.fori_loop(0, (nkc - 1) // 2, pair, carry)
    if (nkc - 1) % 2:
        scores(nkc - 1, 1)
        carry = update(nkc - 2, 0, carry)
        carry = update(nkc - 1, 1, carry)
    else:
        carry = update(nkc - 1, 0, carry)
    accs = [acc / acc[:, den_lane[g]:den_lane[g] + 1] for g, (_, acc) in enumerate(carry)]
    o_ref[...] = _assemble_heads(accs, tq)


def _flash_attention(qz, k, v0, v1, nq_rows, tq, kv_rows, q_blk0=0):
    nd = kv_rows // N2
    kc = N2 * (max(d for d in range(1, 6) if nd % d == 0) if kv_rows % N2 == 0 else 1)
    kvspec = pl.BlockSpec((kv_rows, LANES), lambda i: (0, 0))
    return pl.pallas_call(
        functools.partial(_flash_body, tq=tq, kc=kc, nkc=kv_rows // kc),
        grid=(nq_rows // tq,),
        in_specs=[pl.BlockSpec((AT_H, tq, LANES), lambda i: (0, i + q_blk0, 0)),
                  kvspec, kvspec, kvspec],
        out_specs=pl.BlockSpec((tq, AT_W), lambda i: (i, 0)),
        out_shape=jax.ShapeDtypeStruct((nq_rows, AT_W), F32),
        scratch_shapes=[pltpu.VMEM((2, AT_KV, 2 * tq, kc), F32)],
        compiler_params=_cparams(("arbitrary",), V7X_VMEM_LIMIT),
        name="flash_attention",
    )(qz, k, v0, v1)


def _layernorm(x, g, b):
    mu = jnp.mean(x, axis=-1, keepdims=True)
    xc = x - mu
    var = jnp.mean(xc * xc, axis=-1, keepdims=True)
    return xc * lax.rsqrt(var + EPS) * g + b


def _outproj_body(*refs, with_ctx, nlat_tiles, alpha):
    if with_ctx:
        (hyl_ref, hyc_ref, yf_ref, yb_ref, z_ref, atl_ref, atc_ref, x_ref, g1_ref, sc2_ref, sh2_ref,
         w_ref, nw_ref, lg_ref, lb_ref, x1_ref, h2_ref) = refs
    else:
        (hyl_ref, yf_ref, yb_ref, z_ref, atl_ref, x_ref, g1_ref, sc2_ref, sh2_ref,
         w_ref, nw_ref, lg_ref, lb_ref, x1_ref, h2_ref) = refs
    hy, at = hyl_ref[...], atl_ref[...]
    if with_ctx:
        is_ctx = pl.program_id(0) == nlat_tiles
        hy = jnp.where(is_ctx, hyc_ref[...], hy)
        at = jnp.where(is_ctx, atc_ref[...], at)
    m = _dot(hy.T, w_ref[0:HY_W, :])
    gated = (yf_ref[...] + yb_ref[...]) * _silu(z_ref[...])
    gw = SSD_W // SSD_G
    parts = []
    for g in range(SSD_G):
        gg = gated[:, g * gw:(g + 1) * gw]
        parts.append(gg * lax.rsqrt(jnp.mean(gg * gg, axis=-1, keepdims=True) + EPS))
    ssd = jnp.concatenate(parts, axis=1) * nw_ref[...]
    m = m + _dot(ssd, w_ref[HY_W:HY_W + SSD_W, :]) + _dot(at, w_ref[HY_W + SSD_W:, :])
    x1 = _layernorm(alpha * x_ref[...] + g1_ref[0] * m, lg_ref[...], lb_ref[...])
    x1_ref[...] = x1
    h2_ref[...] = (x1 * (1.0 + sc2_ref[0]) + sh2_ref[0]).astype(BF16)


def _out_proj(hy_lat, hy_ctx, yf, yb, z, at_lat, at_ctx, x_tok, g1, sc2, sh2, w_out, nw, lg, lb,
              *, with_ctx, alpha):
    t = x_tok.shape[0]
    nlat_tiles = hy_lat.shape[1] // TM
    nt = t // TM if with_ctx else nlat_tiles
    last = t // TM - 1
    sel = lambda i: (jnp.where(i == last, 1, 0), 0, 0)
    row = lambda i: (i, 0)
    lat_row = lambda i: (jnp.minimum(i, nlat_tiles - 1), 0)
    lat_col = lambda i: (0, jnp.minimum(i, nlat_tiles - 1))
    vec = pl.BlockSpec((1, 1, D_MODEL), sel)
    specs = [pl.BlockSpec((HY_W, TM), lat_col)]
    args = [hy_lat]
    if with_ctx:
        specs.append(_full((HY_W, TM)))
        args.append(hy_ctx)
    specs += [pl.BlockSpec((TM, SSD_W), row)] * 3 + [pl.BlockSpec((TM, AT_W), lat_row)]
    args += [yf, yb, z, at_lat]
    if with_ctx:
        specs.append(_full((TM, AT_W)))
        args.append(at_ctx)
    specs += [pl.BlockSpec((TM, D_MODEL), row), vec, vec, vec, _full((MIX_W, D_MODEL)),
              _full((1, SSD_W)), _full((1, D_MODEL)), _full((1, D_MODEL))]
    args += [x_tok, g1, sc2, sh2, w_out, nw, lg, lb]
    return pl.pallas_call(
        functools.partial(_outproj_body, with_ctx=with_ctx, nlat_tiles=nlat_tiles, alpha=alpha),
        grid=(nt,),
        in_specs=specs,
        out_specs=[pl.BlockSpec((TM, D_MODEL), row), pl.BlockSpec((TM, D_MODEL), row)],
        out_shape=[jax.ShapeDtypeStruct((nt * TM, D_MODEL), F32),
                   jax.ShapeDtypeStruct((nt * TM, D_MODEL), BF16)],
        compiler_params=_cparams(("arbitrary",), V7X_VMEM_LIMIT),
        name="out_proj",
    )(*args)


HALO = 16


def _ffn_body(h_ref, hp_ref, hn_ref, x1_ref, g2_ref, wup_ref, cw_ref, cb_ref, wdn_ref, lg_ref, lb_ref,
              o_ref, *, nlat_tiles, ntiles, alpha):
    i = pl.program_id(0)
    pv = ((i != 0) & (i != nlat_tiles)).astype(F32)
    nv = ((i != nlat_tiles - 1) & (i != ntiles - 1)).astype(F32)
    hext = jnp.concatenate([hp_ref[...], h_ref[...], hn_ref[...]], axis=0)
    rows = lax.broadcasted_iota(jnp.int32, (TM + 2 * HALO, 1), 0)
    keep = jnp.where(rows < HALO, pv, jnp.where(rows >= TM + HALO, nv, 1.0))
    acc = jnp.zeros((TM, D_MODEL), F32)
    nchunks = D_FF // FF_CHUNK

    def up(c):
        return [jnp.dot(hext, wup_ref[:, o:o + FF_CHUNK], preferred_element_type=F32)
                for o in (c * FF_CHUNK, D_FF + c * FF_CHUNK)]

    def conv(u, o):
        u = u * keep
        cw = cw_ref[:, o:o + FF_CHUNK]
        prev = pltpu.roll(u, 1, 0)[HALO:HALO + TM]
        nxt = pltpu.roll(u, TM + 2 * HALO - 1, 0)[HALO:HALO + TM]
        return (cw[0:1] * prev + cw[1:2] * u[HALO:HALO + TM] + cw[2:3] * nxt
                + cb_ref[:, o:o + FF_CHUNK])

    us = up(0)
    for c in range(nchunks):
        us_next = up(c + 1) if c + 1 < nchunks else None
        c0 = c * FF_CHUNK
        act = (_silu(conv(us[0], c0)) * conv(us[1], D_FF + c0)).astype(BF16)
        acc = acc + jnp.dot(act, wdn_ref[c0:c0 + FF_CHUNK, :], preferred_element_type=F32)
        us = us_next
    o_ref[...] = _layernorm(alpha * x1_ref[...] + g2_ref[0] * acc, lg_ref[...], lb_ref[...])


def _conv_ffn(h2, x1, g2, w_up, conv_w, conv_b, w_down, lg, lb, *, nlat_tiles, ntiles, alpha):
    hb = TM // HALO
    lastb = ntiles * hb - 1
    row = lambda i: (i, 0)
    sel = lambda i: (jnp.where(i == nlat_tiles, 1, 0), 0, 0)
    return pl.pallas_call(
        functools.partial(_ffn_body, nlat_tiles=nlat_tiles, ntiles=ntiles, alpha=alpha),
        grid=(ntiles,),
        in_specs=[pl.BlockSpec((TM, D_MODEL), row),
                  pl.BlockSpec((HALO, D_MODEL), lambda i: (jnp.maximum(i * hb - 1, 0), 0)),
                  pl.BlockSpec((HALO, D_MODEL), lambda i: (jnp.minimum((i + 1) * hb, lastb), 0)),
                  pl.BlockSpec((TM, D_MODEL), row),
                  pl.BlockSpec((1, 1, D_MODEL), sel),
                  _full((D_MODEL, 2 * D_FF)), _full((3, 2 * D_FF)), _full((1, 2 * D_FF)),
                  _full((D_FF, D_MODEL)), _full((1, D_MODEL)), _full((1, D_MODEL))],
        out_specs=pl.BlockSpec((TM, D_MODEL), row),
        out_shape=jax.ShapeDtypeStruct((ntiles * TM, D_MODEL), F32),
        compiler_params=_cparams(("arbitrary",), V7X_VMEM_LIMIT),
        name="conv_ffn",
    )(h2, h2, h2, x1, g2, w_up, conv_w, conv_b, w_down, lg, lb)


def _rope_tables(nlat):
    rows = nlat // GRID_W
    row = jnp.repeat(jnp.arange(rows, dtype=F32), GRID_W)
    col = jnp.tile(jnp.arange(GRID_W, dtype=F32), rows)
    nf = AT_D // 4
    inv = ROPE_BASE ** (-jnp.arange(nf, dtype=F32) / nf)
    ang = jnp.concatenate([row[:, None] * inv, col[:, None] * inv], axis=-1)
    cos, sin = jnp.cos(ang), jnp.sin(ang)
    cos_h = jnp.concatenate([cos, cos], axis=1)
    sin_h = jnp.concatenate([-sin, sin], axis=1)
    cos_t = jnp.concatenate([jnp.tile(cos_h, (1, LANES // AT_D)), jnp.ones((CTX, LANES), F32)], axis=0)
    sin_t = jnp.concatenate([jnp.tile(sin_h, (1, LANES // AT_D)), jnp.zeros((CTX, LANES), F32)], axis=0)
    return cos_t, sin_t


def _expansion():
    w = SSD_R * SSD_P
    ex = np.zeros((2 * SSD_G, 4 * LANES, 2 * w), np.float32)
    for d in range(2):
        for g in range(SSD_G):
            e = np.zeros((LANES, w), np.float32)
            for r in range(SSD_R):
                e[d * SSD_H + g * SSD_R + r, r * SSD_P:(r + 1) * SSD_P] = 1.0
            for piece in range(4):
                ex[d * SSD_G + g, piece * LANES:(piece + 1) * LANES,
                   (piece // 2) * w:(piece // 2 + 1) * w] = e
    return jnp.asarray(ex, BF16)


def _group_mean_matrix():
    w = AT_W + AT_KVW
    idx = np.arange(w) // AT_D
    return jnp.asarray((idx[:, None] == idx[None, :]).astype(np.float32) / AT_D)


def _pad_lanes(v, width=LANES):
    return jnp.pad(v, ((0, 0), (0, width - v.shape[1])))


def kernel(x, c, ctx, c_ctx, w_mod, b_mod, w_in, hy_conv_w, hy_conv_b, hy_ffn_w1, hy_ffn_b1, hy_ffn_w2, hy_ffn_b2, hy_ffn_w3, hy_freq, hy_bias, ssd_conv_w, ssd_conv_b, ssd_dt_bias, ssd_a_log, ssd_d, ssd_norm_w, attn_q_norm, attn_k_norm, w_out, ln1_g, ln1_b, ffn_w_up, ffn_conv_w, ffn_conv_b, ffn_w_down, ln2_g, ln2_b):
    nlat = x.shape[1]
    assert x.shape[0] == 1 and ctx.shape[1] == CTX and nlat % (N2 * 8) == 0
    t = nlat + CTX
    nlt = nlat // TM
    alpha = (2.0 * DEPTH) ** 0.25
    cb = 16

    x_tok = jnp.concatenate([x[0], ctx[0]], axis=0)
    cvec = jnp.zeros((8, D_MODEL), F32).at[0].set(c[0]).at[1].set(c_ctx)
    mod = _modulation(cvec, w_mod, b_mod)
    cos_t, sin_t = _rope_tables(nlat)
    ex = _expansion()
    gsum = _group_mean_matrix()
    tb = _fft_tables(nlat)

    for i in range(DEPTH):
        ctx_out = i < DEPTH - 1
        mv = mod[i, 0:2].reshape(2, 6, 1, D_MODEL)
        sh1, sc1, g1, sh2, sc2, g2 = (mv[:, q] for q in range(6))

        wi = w_in[i]
        o = HY_COLS
        w_z = wi[:, o:o + SSD_W]
        w_xbc = wi[:, o + SSD_W:o + SSD_W + SSD_XBC]
        w_dt = _pad_lanes(wi[:, o + SSD_W + SSD_XBC:o + SSD_W + SSD_XBC + 2 * SSD_H])
        w_at = wi[:, o + SSD_W + SSD_XBC + 2 * SSD_H:]
        w_main = jnp.concatenate([w_z, w_xbc, w_dt, w_at], axis=1).astype(BF16)
        w_hyT = wi[:, 0:HY_COLS].T.astype(BF16)

        hyT, z, xbc, dt, qkv = _in_proj(x_tok, sc1, sh1, w_main, w_hyT)

        w1t = _pad_lanes(hy_ffn_w1[i].T)
        fargs = (w1t, hy_ffn_b1[i][:, None], hy_ffn_w2[i].T, hy_ffn_b2[i][:, None],
                 hy_ffn_w3[i].T, hy_freq[i][:, None])
        taps2d = jnp.concatenate([hy_conv_w[i].T, hy_conv_b[i][:, None]], axis=1)
        taps = taps2d.reshape(-1)
        skip = hy_bias[i].reshape(-1)
        filt = _hyena_filters(nlat, *fargs).reshape(4, HY_W, tb["r1"], N2)
        kf = _filter_spectra(filt, tb, cb)
        hy3 = hyT.reshape(HY_COLS, t // N2, N2)
        zz = _hyena_conv(taps, skip, hy3, 0, hy3, HY_W // cb, kf, tb, cb,
                         conv_u=True, u_off=0, g_off=HY_W, order=0)
        y_hy = _hyena_conv(taps, skip, zz, 0, hy3, 2 * HY_W // cb, kf, tb, cb,
                           conv_u=False, u_off=0, g_off=2 * HY_W, order=1)
        y_hy = y_hy.reshape(HY_W, nlat)
        if ctx_out:
            filt_c = _hyena_filters(CTX, *fargs)
            y_hy_c = _hyena_ctx(hyT, nlat // CTX, filt_c, taps2d, skip[:, None])
        else:
            y_hy_c = None

        d_exp = jnp.repeat(ssd_d[i], SSD_P)[None, :]
        yf, yb = _ssd(xbc, dt, ssd_conv_w[i], ssd_conv_b[i][None, :],
                      _pad_lanes(ssd_dt_bias[i].reshape(1, -1)), _pad_lanes(ssd_a_log[i].reshape(1, -1)),
                      d_exp, ex, nlat // CHUNK)

        qz, kk, v0, v1 = _attn_prep(qkv, cos_t, sin_t, jnp.tile(attn_q_norm[i], AT_H)[None, :],
                                    jnp.tile(attn_k_norm[i], AT_KV)[None, :], gsum)
        y_at = _flash_attention(qz, kk, v0, v1, nlat, 256, t)
        if ctx_out:
            y_at_c = _flash_attention(qz[:, nlat:], kk[nlat:], v0[nlat:], v1[nlat:], CTX, 128, CTX)
        else:
            y_at_c = None

        x1, h2 = _out_proj(y_hy, y_hy_c, yf, yb, z, y_at, y_at_c, x_tok, g1, sc2, sh2,
                           w_out[i].astype(BF16), ssd_norm_w[i][None, :], ln1_g[i][None, :],
                           ln1_b[i][None, :], with_ctx=ctx_out, alpha=alpha)
        ntiles = nlt + 1 if ctx_out else nlt
        x_tok = _conv_ffn(h2, x1, g2, ffn_w_up[i].astype(BF16), ffn_conv_w[i], ffn_conv_b[i][None, :],
                          ffn_w_down[i].astype(BF16), ln2_g[i][None, :], ln2_b[i][None, :],
                          nlat_tiles=nlt, ntiles=ntiles, alpha=alpha)
    return x_tok[None]
```

```python
import functools
import math

import numpy as np
import jax
import jax.numpy as jnp
from jax import lax
from jax.experimental import pallas as pl
from jax.experimental.pallas import tpu as pltpu

F32 = jnp.float32
BF16 = jnp.bfloat16
HI = lax.Precision.HIGHEST

D_MODEL = 1024
DEPTH = 2
GRID_W = 64
CTX = 256
EPS = 1e-6

HY_W = 256
HY_BANDS = 16
HY_EMB = 1 + 2 * HY_BANDS
HY_HID = 64
HY_MIN_DECAY = math.log(1e-2) / 1.5
HY_MAX_DECAY = math.log(1e-2) / 0.3
HY_COLS = 3 * HY_W

SSD_H = 8
SSD_P = 64
SSD_W = SSD_H * SSD_P
SSD_G = 2
SSD_R = SSD_H // SSD_G
SSD_N = 128
SSD_XBC = SSD_W + 2 * SSD_G * SSD_N
CHUNK = 128

AT_H = 4
AT_KV = 2
AT_D = 64
AT_W = AT_H * AT_D
AT_KVW = AT_KV * AT_D
ROPE_BASE = 10000.0

MIX_W = HY_W + SSD_W + AT_W
D_FF = 2816
FF_CHUNK = 256

TM = CTX
LANES = 128
N2 = 256
V7X_VMEM_LIMIT = 56 * 1024 * 1024


def _cparams(sem, vmem=None):
    return pltpu.CompilerParams(dimension_semantics=sem, vmem_limit_bytes=vmem)


def _dot(a, b):
    return jnp.dot(a.astype(BF16), b.astype(BF16), preferred_element_type=F32)


def _dot_hi(a, b):
    return jnp.dot(a, b, preferred_element_type=F32, precision=HI)


def _split(x, n):
    pieces = []
    for _ in range(n):
        p = x.astype(BF16)
        pieces.append(p)
        x = x - p.astype(F32)
    return pieces


def _sigmoid(x):
    return 1.0 / (1.0 + jnp.exp(-x))


def _silu(x):
    return x * _sigmoid(x)


def _full(shape):
    nd = len(shape)
    return pl.BlockSpec(shape, lambda *_: (0,) * nd)


def _mod_body(c_ref, w_ref, b_ref, o_ref):
    s = _silu(c_ref[...])
    o_ref[0] = _dot(s, w_ref[0]) + b_ref[0]


def _modulation(cvec, w_mod, b_mod):
    depth, d, n = w_mod.shape
    nb = 1536
    return pl.pallas_call(
        _mod_body,
        grid=(depth, n // nb),
        in_specs=[pl.BlockSpec((8, d), lambda l, j: (0, 0)),
                  pl.BlockSpec((1, d, nb), lambda l, j: (l, 0, j)),
                  pl.BlockSpec((1, 1, nb), lambda l, j: (l, 0, j))],
        out_specs=pl.BlockSpec((1, 8, nb), lambda l, j: (l, 0, j)),
        out_shape=jax.ShapeDtypeStruct((depth, 8, n), F32),
        compiler_params=_cparams(("arbitrary", "arbitrary")),
        name="modulation",
    )(cvec, w_mod, b_mod.reshape(depth, 1, n))


W_MAIN = SSD_W + SSD_XBC + LANES + AT_W + 2 * AT_KVW


def _inproj_body(x_ref, sc_ref, sh_ref, wm_ref, wh_ref, hy_ref, z_ref, xbc_ref, dt_ref, qkv_ref):
    h = (x_ref[...] * (1.0 + sc_ref[0]) + sh_ref[0]).astype(BF16)
    main = jnp.dot(h, wm_ref[...], preferred_element_type=F32)
    z_ref[...] = main[:, 0:SSD_W]
    xbc_ref[...] = main[:, SSD_W:SSD_W + SSD_XBC]
    o = SSD_W + SSD_XBC
    dt_ref[...] = main[:, o:o + LANES]
    qkv_ref[...] = main[:, o + LANES:]
    hy_ref[...] = lax.dot_general(wh_ref[...], h, (((1,), (1,)), ((), ())),
                                  preferred_element_type=F32)


def _in_proj(x_tok, sc, sh, w_main, w_hyT):
    t = x_tok.shape[0]
    nt = t // TM
    sel = lambda i: (jnp.where(i == nt - 1, 1, 0), 0, 0)
    row = lambda i: (i, 0)
    return pl.pallas_call(
        _inproj_body,
        grid=(nt,),
        in_specs=[pl.BlockSpec((TM, D_MODEL), row),
                  pl.BlockSpec((1, 1, D_MODEL), sel),
                  pl.BlockSpec((1, 1, D_MODEL), sel),
                  _full((D_MODEL, W_MAIN)),
                  _full((HY_COLS, D_MODEL))],
        out_specs=[pl.BlockSpec((HY_COLS, TM), lambda i: (0, i)),
                   pl.BlockSpec((TM, SSD_W), row),
                   pl.BlockSpec((TM, SSD_XBC), row),
                   pl.BlockSpec((TM, LANES), row),
                   pl.BlockSpec((TM, AT_W + 2 * AT_KVW), row)],
        out_shape=[jax.ShapeDtypeStruct((HY_COLS, t), F32),
                   jax.ShapeDtypeStruct((t, SSD_W), F32),
                   jax.ShapeDtypeStruct((t, SSD_XBC), F32),
                   jax.ShapeDtypeStruct((t, LANES), F32),
                   jax.ShapeDtypeStruct((t, AT_W + 2 * AT_KVW), F32)],
        compiler_params=_cparams(("arbitrary",), V7X_VMEM_LIMIT),
        name="in_proj",
    )(x_tok, sc, sh, w_main, w_hyT)


def _filter_body(w1_ref, b1_ref, w2_ref, b2_ref, w3_ref, fr_ref, o_ref, *, n, nt):
    j = pl.program_id(0)
    k = (lax.broadcasted_iota(jnp.int32, (1, nt), 1) + j * nt).astype(F32)
    t = k * (1.0 / (n - 1))
    wk = k * (2.0 * math.pi / n)
    band = lax.broadcasted_iota(jnp.int32, (HY_BANDS, nt), 0).astype(F32)
    arg = (1e-4 + band * ((HY_BANDS - 1 - 1e-4) / (HY_BANDS - 1))) * wk
    r8 = lax.broadcasted_iota(jnp.int32, (8, nt), 0)
    feats = jnp.concatenate([jnp.cos(arg), -jnp.sin(arg), jnp.where(r8 == 0, t, 0.0),
                             jnp.zeros((LANES - 2 * HY_BANDS - 8, nt), F32)], axis=0)
    fr = fr_ref[...]
    h = jnp.sin(fr * (_dot_hi(w1_ref[...], feats) + b1_ref[...]))
    h = jnp.sin(fr * (_dot_hi(w2_ref[...], h) + b2_ref[...]))
    h = _dot(w3_ref[...], h)
    c = lax.broadcasted_iota(jnp.int32, (HY_W, nt), 0).astype(F32)
    delta = jnp.abs(HY_MIN_DECAY + c * ((HY_MAX_DECAY - HY_MIN_DECAY) / (HY_W - 1)))
    window = jnp.exp(-t * delta)
    for q in range(4):
        o_ref[q] = h[q * HY_W:(q + 1) * HY_W] * window


def _hyena_filters(n, w1t, b1, w2t, b2, w3t, freq):
    nt = min(n, 2048)
    return pl.pallas_call(
        functools.partial(_filter_body, n=n, nt=nt),
        grid=(n // nt,),
        in_specs=[_full((HY_HID, LANES)), _full((HY_HID, 1)), _full((HY_HID, HY_HID)),
                  _full((HY_HID, 1)), _full((4 * HY_W, HY_HID)), _full((HY_HID, 1))],
        out_specs=pl.BlockSpec((4, HY_W, nt), lambda j: (0, 0, j)),
        out_shape=jax.ShapeDtypeStruct((4, HY_W, n), F32),
        compiler_params=_cparams(("arbitrary",)),
        name="hyena_filters",
    )(w1t, b1, w2t, b2, w3t, freq)


def _fft_tables(n):
    big = 2 * n
    n1 = big // N2
    r1 = n1 // 2
    kp = -(-(r1 + 1) // 8) * 8
    k1 = np.arange(kp)[:, None].astype(np.float64)
    live = (k1 <= r1)
    a1 = 2 * np.pi * k1 * np.arange(r1)[None, :] / n1
    f1a = np.concatenate([np.cos(a1) * live, -np.sin(a1) * live], axis=0)
    at = 2 * np.pi * k1 * np.arange(N2)[None, :] / big
    twr, twi = np.cos(at) * live, -np.sin(at) * live
    a2 = 2 * np.pi * np.outer(np.arange(N2), np.arange(N2)) / N2
    c2, s2 = np.cos(a2), np.sin(a2)
    w2f = np.block([[c2, -s2], [s2, c2]])
    w2i = np.block([[c2, s2], [-s2, c2]])
    wgt = np.where((k1 == 0) | (k1 == r1), 1.0, 2.0) * live / big
    f1i = np.concatenate([(np.cos(a1) * wgt).T, (-np.sin(a1) * wgt).T], axis=1)
    j = jnp.asarray
    return dict(r1=r1, kp=kp, f1a=j(f1a, BF16), twr=j(twr, F32), twi=j(twi, F32),
                w2f=j(w2f, BF16), w2i=j(w2i, BF16), f1i=j(f1i, BF16))


def _fft_rows(x, f1a_ref, twr_ref, twi_ref, kp):
    a = jnp.dot(f1a_ref[...], x.astype(BF16), preferred_element_type=F32)
    ar, ai = a[:kp], a[kp:]
    twr, twi = twr_ref[...], twi_ref[...]
    return jnp.concatenate([ar * twr - ai * twi, ar * twi + ai * twr], axis=1)


def _kspec_body(hf_ref, hb_ref, f1a_ref, twr_ref, twi_ref, w2f_ref, o_ref, sa_ref, *, cb, kp, r1):
    row = lax.broadcasted_iota(jnp.int32, (r1, N2), 0)
    lane = lax.broadcasted_iota(jnp.int32, (r1, N2), 1)
    first = (row == 0) & (lane == 0)

    for c in range(cb):
        af = _fft_rows(hf_ref[0, c], f1a_ref, twr_ref, twi_ref, kp)
        ab = _fft_rows(jnp.where(first, 0.0, hb_ref[0, c]), f1a_ref, twr_ref, twi_ref, kp)
        sa_ref[c * kp:(c + 1) * kp, :] = af + ab
        sa_ref[(cb + c) * kp:(cb + c + 1) * kp, :] = af - ab
    kr = jnp.dot(sa_ref[0:cb * kp, :].astype(BF16), w2f_ref[:, 0:N2], preferred_element_type=F32)
    ki = jnp.dot(sa_ref[cb * kp:, :].astype(BF16), w2f_ref[:, N2:], preferred_element_type=F32)
    o_ref[0] = jnp.concatenate([kr, ki], axis=1).reshape(cb, kp, 2 * N2)


def _filter_spectra(filt, tb, cb):
    r1, kp = tb["r1"], tb["kp"]
    return pl.pallas_call(
        functools.partial(_kspec_body, cb=cb, kp=kp, r1=r1),
        grid=(2, HY_W // cb),
        in_specs=[pl.BlockSpec((1, cb, r1, N2), lambda o, j: (2 * o, j, 0, 0)),
                  pl.BlockSpec((1, cb, r1, N2), lambda o, j: (2 * o + 1, j, 0, 0)),
                  _full((2 * kp, r1)), _full((kp, N2)), _full((kp, N2)), _full((2 * N2, 2 * N2))],
        out_specs=pl.BlockSpec((1, cb, kp, 2 * N2), lambda o, j: (o, j, 0, 0)),
        out_shape=jax.ShapeDtypeStruct((2, HY_W, kp, 2 * N2), F32),
        scratch_shapes=[pltpu.VMEM((2 * cb * kp, 2 * N2), F32)],
        compiler_params=_cparams(("arbitrary", "arbitrary"), V7X_VMEM_LIMIT),
        name="hyena_filter_spectra",
    )(filt, filt, tb["f1a"], tb["twr"], tb["twi"], tb["w2f"])


def _short_conv(x, w0, w1, w2, b):
    r1 = x.shape[0]
    row = lax.broadcasted_iota(jnp.int32, x.shape, 0)
    lane = lax.broadcasted_iota(jnp.int32, x.shape, 1)
    a = pltpu.roll(x, 1, 1)
    prev = jnp.where(lane == 0, jnp.where(row == 0, 0.0, pltpu.roll(a, 1, 0)), a)
    a = pltpu.roll(x, N2 - 1, 1)
    nxt = jnp.where(lane == N2 - 1, jnp.where(row == r1 - 1, 0.0, pltpu.roll(a, r1 - 1, 0)), a)
    return w0 * prev + w1 * x + w2 * nxt + b


def _hyconv_body(taps_ref, skip_ref, u_ref, g_ref, kf_ref, f1a_ref, twr_ref, twi_ref, w2f_ref,
                 w2i_ref, f1i_ref, o_ref, sa_ref, sq_ref, sx_ref, *, cb, kp, conv_u, u_off,
                 g_off, order):
    j = pl.program_id(0)

    def taps(ch):
        return (taps_ref[ch * 4], taps_ref[ch * 4 + 1], taps_ref[ch * 4 + 2], taps_ref[ch * 4 + 3])

    for c in range(cb):
        x = u_ref[c]
        if conv_u:
            x = _short_conv(x, *taps(u_off + j * cb + c))
        sx_ref[c] = x
        sa_ref[c * kp:(c + 1) * kp, :] = _fft_rows(x, f1a_ref, twr_ref, twi_ref, kp)
    x = jnp.dot(sa_ref[...].astype(BF16), w2f_ref[...], preferred_element_type=F32)
    k = kf_ref[0].reshape(cb * kp, 2 * N2)
    xr, xi, kr, ki = x[:, :N2], x[:, N2:], k[:, :N2], k[:, N2:]
    p = jnp.concatenate([xr * kr - xi * ki, xr * ki + xi * kr], axis=1).astype(BF16)
    sq_ref[...] = jnp.dot(p, w2i_ref[...], preferred_element_type=F32)

    for c in range(cb):
        q = sq_ref[c * kp:(c + 1) * kp, :]
        qr, qi = q[:, :N2], q[:, N2:]
        twr, twi = twr_ref[...], twi_ref[...]
        y2 = jnp.concatenate([qr * twr + qi * twi, qi * twr - qr * twi], axis=0).astype(BF16)
        y = jnp.dot(f1i_ref[...], y2, preferred_element_type=F32)
        y = y + skip_ref[order * HY_W + j * cb + c] * sx_ref[c]
        o_ref[c] = _short_conv(g_ref[c], *taps(g_off + j * cb + c)) * y


def _hyena_conv(taps, skip, u3, u_blk0, g3, g_blk0, kf, tb, cb, *, conv_u, u_off, g_off, order):
    r1, kp = tb["r1"], tb["kp"]
    smem = pl.BlockSpec(memory_space=pltpu.SMEM)
    return pl.pallas_call(
        functools.partial(_hyconv_body, cb=cb, kp=kp, conv_u=conv_u, u_off=u_off, g_off=g_off,
                          order=order),
        grid=(HY_W // cb,),
        in_specs=[smem, smem,
                  pl.BlockSpec((cb, r1, N2), lambda j: (j + u_blk0, 0, 0)),
                  pl.BlockSpec((cb, r1, N2), lambda j: (j + g_blk0, 0, 0)),
                  pl.BlockSpec((1, cb, kp, 2 * N2), lambda j: (order, j, 0, 0)),
                  _full((2 * kp, r1)), _full((kp, N2)), _full((kp, N2)),
                  _full((2 * N2, 2 * N2)), _full((2 * N2, 2 * N2)), _full((r1, 2 * kp))],
        out_specs=pl.BlockSpec((cb, r1, N2), lambda j: (j, 0, 0)),
        out_shape=jax.ShapeDtypeStruct((HY_W, r1, N2), F32),
        scratch_shapes=[pltpu.VMEM((cb * kp, 2 * N2), F32),
                        pltpu.VMEM((cb * kp, 2 * N2), F32),
                        pltpu.VMEM((cb, r1, N2), F32)],
        compiler_params=_cparams(("arbitrary",), V7X_VMEM_LIMIT),
        name="hyena_conv%d" % order,
    )(taps, skip, u3, g3, kf, tb["f1a"], tb["twr"], tb["twi"], tb["w2f"], tb["w2i"], tb["f1i"])


def _ctx_tables(n):
    big = 2 * n
    a = 2 * np.pi * np.outer(np.arange(n), np.arange(big)) / big
    fc = np.concatenate([np.cos(a), -np.sin(a)], axis=1)
    gi = np.concatenate([np.cos(a).T, -np.sin(a).T], axis=0) / big
    return jnp.asarray(fc, BF16), jnp.asarray(gi, BF16)


def _hyctx_body(hy_ref, filt_ref, taps_ref, skip_ref, fc_ref, gi_ref, o_ref, *, n):
    big = 2 * n
    lane = lax.broadcasted_iota(jnp.int32, (HY_W, n), 1)

    def sconv(x, t):
        prev = jnp.where(lane == 0, 0.0, pltpu.roll(x, 1, 1))
        nxt = jnp.where(lane == n - 1, 0.0, pltpu.roll(x, n - 1, 1))
        return t[:, 0:1] * prev + t[:, 1:2] * x + t[:, 2:3] * nxt + t[:, 3:4]

    def spec(o):
        xf = _dot(filt_ref[2 * o], fc_ref[...])
        xb = _dot(jnp.where(lane == 0, 0.0, filt_ref[2 * o + 1]), fc_ref[...])
        return xf[:, :big] + xb[:, :big], xf[:, big:] - xb[:, big:]

    def lconv(u, o):
        kr, ki = spec(o)
        x = _dot(u, fc_ref[...])
        xr, xi = x[:, :big], x[:, big:]
        p = jnp.concatenate([xr * kr - xi * ki, xr * ki + xi * kr], axis=1)
        return _dot(p, gi_ref[...]) + skip_ref[o * HY_W:(o + 1) * HY_W, :] * u

    v = sconv(hy_ref[0:HY_W, :], taps_ref[0:HY_W, :])
    x1 = sconv(hy_ref[HY_W:2 * HY_W, :], taps_ref[HY_W:2 * HY_W, :])
    x2 = sconv(hy_ref[2 * HY_W:, :], taps_ref[2 * HY_W:, :])
    z = x1 * lconv(v, 0)
    o_ref[...] = x2 * lconv(z, 1)


def _hyena_ctx(hyT, blk, filt_ctx, taps2d, skip2d):
    n = filt_ctx.shape[-1]
    fc, gi = _ctx_tables(n)
    return pl.pallas_call(
        functools.partial(_hyctx_body, n=n),
        grid=(1,),
        in_specs=[pl.BlockSpec((HY_COLS, n), lambda i: (0, blk)),
                  _full((4, HY_W, n)), _full((HY_COLS, 4)), _full((2 * HY_W, 1)),
                  _full((n, 4 * n)), _full((4 * n, n))],
        out_specs=_full((HY_W, n)),
        out_shape=jax.ShapeDtypeStruct((HY_W, n), F32),
        compiler_params=_cparams(("arbitrary",), V7X_VMEM_LIMIT),
        name="hyena_ctx",
    )(hyT, filt_ctx, taps2d, skip2d, fc, gi)


def _ssd_body(xf_ref, xfp_ref, xfn_ref, dtf_ref, xb_ref, xbp_ref, xbn_ref, dtb_ref,
              cw_ref, cb_ref, dtbias_ref, alog_ref, dexp_ref, ex_ref,
              yf_ref, yb_ref, h_ref, sx_ref, *, nlat):
    j = pl.program_id(0)
    nch = nlat + CTX // CHUNK

    @pl.when(j == 0)
    def _():
        h_ref[...] = jnp.zeros_like(h_ref)

    ri = lax.broadcasted_iota(jnp.int32, (CHUNK, CHUNK), 0)
    ci = lax.broadcasted_iota(jnp.int32, (CHUNK, CHUNK), 1)
    a_all = -jnp.exp(alog_ref[...])

    for d in range(2):
        x_ref, xp_ref, xn_ref, dt_ref, y_ref = (
            (xf_ref, xfp_ref, xfn_ref, dtf_ref, yf_ref) if d == 0
            else (xb_ref, xbp_ref, xbn_ref, dtb_ref, yb_ref))
        cid = jnp.where(j < 2, nlat + j, j - 2) if d == 0 else nch - 1 - j
        pv = ((cid != 0) & (cid != nlat)).astype(F32)
        nv = ((cid != nlat - 1) & (cid != nch - 1)).astype(F32)
        sx_ref[0:8, :] = xp_ref[...] * pv
        sx_ref[8:8 + CHUNK, :] = x_ref[...]
        sx_ref[8 + CHUNK:, :] = xn_ref[...] * nv
        pre = (cw_ref[0:1, :] * sx_ref[pl.ds(7, CHUNK), :] + cw_ref[1:2, :] * x_ref[...]
               + cw_ref[2:3, :] * sx_ref[pl.ds(9, CHUNK), :] + cb_ref[...])
        xc = _silu(pre)
        xs = xc[:, 0:SSD_W]
        z = dt_ref[...] + dtbias_ref[...]
        dtv = jnp.maximum(z, 0.0) + jnp.log1p(jnp.exp(-jnp.abs(z)))
        da = dtv * a_all
        tri = (ri >= ci) if d == 0 else (ri <= ci)
        r3 = jnp.dot(tri.astype(BF16), jnp.concatenate(_split(da, 3), axis=1),
                     preferred_element_type=F32)
        acum = r3[:, 0:LANES] + r3[:, LANES:2 * LANES] + r3[:, 2 * LANES:]
        last = CHUNK - 1 if d == 0 else 0
        acum_t = acum.T
        pieces = jnp.concatenate(_split(acum, 2) + _split(dtv, 2), axis=1)
        ys = []
        for g in range(SSD_G):
            bm = xc[:, SSD_W + g * SSD_N:SSD_W + (g + 1) * SSD_N]
            cm = xc[:, SSD_W + (SSD_G + g) * SSD_N:SSD_W + (SSD_G + g + 1) * SSD_N]
            bt = bm.T
            scores = _dot(cm, bt)
            both = jnp.dot(pieces, ex_ref[d * SSD_G + g], preferred_element_type=F32)
            acum_e, dt_e = both[:, 0:SSD_R * SSD_P], both[:, SSD_R * SSD_P:]
            tot_e = acum_e[last:last + 1, :]
            xg = xs[:, g * SSD_R * SSD_P:(g + 1) * SSD_R * SSD_P]
            hg = h_ref[d, g]
            yoff = _dot(cm, hg) * jnp.exp(acum_e)
            ydiag = []
            for r in range(SSD_R):
                hl = d * SSD_H + g * SSD_R + r
                seg = acum[:, hl:hl + 1] - acum_t[hl:hl + 1, :]
                m = scores * jnp.exp(jnp.where(tri, seg, -jnp.inf))
                xdt = xg[:, r * SSD_P:(r + 1) * SSD_P] * dt_e[:, r * SSD_P:(r + 1) * SSD_P]
                ydiag.append(_dot(m, xdt))
            ys.append(jnp.concatenate(ydiag, axis=1) + yoff)
            xw = xg * (jnp.exp(tot_e - acum_e) * dt_e)
            h_ref[d, g] = hg * jnp.exp(tot_e) + _dot(bt, xw)
        y = jnp.concatenate(ys, axis=1)
        if d == 0:
            y = y + dexp_ref[...] * xs
        y_ref[...] = y


def _ssd(xbc, dt, conv_w, conv_b, dt_bias, a_log, d_exp, ex, nlat):
    t = xbc.shape[0]
    nch = t // CHUNK
    hb = CHUNK // 8
    last8 = t // 8 - 1
    fc = lambda j: jnp.where(j < 2, nlat + j, j - 2)
    bc = lambda j: nch - 1 - j
    prev = lambda f: (lambda j: (jnp.maximum(f(j) * hb - 1, 0), 0))
    nxt = lambda f: (lambda j: (jnp.minimum((f(j) + 1) * hb, last8), 0))
    main = lambda f: (lambda j: (f(j), 0))
    return pl.pallas_call(
        functools.partial(_ssd_body, nlat=nlat),
        grid=(nch,),
        in_specs=[pl.BlockSpec((CHUNK, SSD_XBC), main(fc)), pl.BlockSpec((8, SSD_XBC), prev(fc)),
                  pl.BlockSpec((8, SSD_XBC), nxt(fc)), pl.BlockSpec((CHUNK, LANES), main(fc)),
                  pl.BlockSpec((CHUNK, SSD_XBC), main(bc)), pl.BlockSpec((8, SSD_XBC), prev(bc)),
                  pl.BlockSpec((8, SSD_XBC), nxt(bc)), pl.BlockSpec((CHUNK, LANES), main(bc)),
                  _full((3, SSD_XBC)), _full((1, SSD_XBC)), _full((1, LANES)), _full((1, LANES)),
                  _full((1, SSD_W)), _full((2 * SSD_G, 4 * LANES, 2 * SSD_R * SSD_P))],
        out_specs=[pl.BlockSpec((CHUNK, SSD_W), main(fc)), pl.BlockSpec((CHUNK, SSD_W), main(bc))],
        out_shape=[jax.ShapeDtypeStruct((t, SSD_W), F32), jax.ShapeDtypeStruct((t, SSD_W), F32)],
        scratch_shapes=[pltpu.VMEM((2, SSD_G, SSD_N, SSD_R * SSD_P), F32),
                        pltpu.VMEM((CHUNK + 16, SSD_XBC), F32)],
        compiler_params=_cparams(("arbitrary",), V7X_VMEM_LIMIT),
        name="ssd_scan",
    )(xbc, xbc, xbc, dt, xbc, xbc, xbc, dt, conv_w, conv_b, dt_bias, a_log, d_exp, ex)


def _attn_prep_body(qkv_ref, cos_ref, sin_ref, qn_ref, kn_ref, gsum_ref, q_ref, k_ref, v0_ref, v1_ref):
    x = qkv_ref[...]
    qk = x[:, 0:AT_W + AT_KVW]
    ms = jnp.dot(jnp.concatenate(_split(qk * qk, 2), axis=1), gsum_ref[...],
                 preferred_element_type=F32)
    wn = jnp.concatenate([qn_ref[...], kn_ref[...]], axis=1)
    qk = qk * lax.rsqrt(ms + EPS) * wn
    w = qk.shape[1]
    lane = lax.broadcasted_iota(jnp.int32, qk.shape, 1)
    partner = jnp.where(lane % AT_D < AT_D // 2, pltpu.roll(qk, w - AT_D // 2, 1),
                        pltpu.roll(qk, AT_D // 2, 1))
    cos = jnp.concatenate([cos_ref[...]] * 3, axis=1)
    sin = jnp.concatenate([sin_ref[...]] * 3, axis=1)
    qk = qk * cos + partner * sin
    q = qk[:, 0:AT_W] * (AT_D ** -0.5 * math.log2(math.e))
    zeros = jnp.zeros((q.shape[0], AT_D), F32)
    for h in range(AT_H):
        qh = q[:, h * AT_D:(h + 1) * AT_D]
        qh = (jnp.concatenate([qh, zeros], axis=1) if h // 2 == 0
              else jnp.concatenate([zeros, qh], axis=1))
        q_ref[h] = qh.T.astype(BF16)
    k_ref[...] = qk[:, AT_W:].astype(BF16)
    v = x[:, AT_W + AT_KVW:]
    vl = lax.broadcasted_iota(jnp.int32, v.shape, 1)
    v0_ref[0] = jnp.where(vl < AT_D, v, jnp.where(vl == AT_D, 1.0, 0.0)).T.astype(BF16)
    v1_ref[0] = jnp.where(vl >= AT_D, v, jnp.where(vl == 0, 1.0, 0.0)).T.astype(BF16)


def _attn_prep(qkv, cos_t, sin_t, qn, kn, gsum):
    t = qkv.shape[0]
    row = lambda i: (i, 0)
    return pl.pallas_call(
        _attn_prep_body,
        grid=(t // TM,),
        in_specs=[pl.BlockSpec((TM, AT_W + 2 * AT_KVW), row), pl.BlockSpec((TM, LANES), row),
                  pl.BlockSpec((TM, LANES), row), _full((1, AT_W)), _full((1, AT_KVW)),
                  _full((2 * (AT_W + AT_KVW), AT_W + AT_KVW))],
        out_specs=[pl.BlockSpec((AT_H, LANES, TM), lambda i: (0, 0, i)),
                   pl.BlockSpec((TM, LANES), row),
                   pl.BlockSpec((1, LANES, TM), lambda i: (i, 0, 0)),
                   pl.BlockSpec((1, LANES, TM), lambda i: (i, 0, 0))],
        out_shape=[jax.ShapeDtypeStruct((AT_H, LANES, t), BF16),
                   jax.ShapeDtypeStruct((t, LANES), BF16),
                   jax.ShapeDtypeStruct((t // TM, LANES, TM), BF16),
                   jax.ShapeDtypeStruct((t // TM, LANES, TM), BF16)],
        compiler_params=_cparams(("arbitrary",)),
        name="attn_prep",
    )(qkv, cos_t, sin_t, qn, kn, gsum)


def _assemble_heads(accs, tq):
    lane = lax.broadcasted_iota(jnp.int32, (tq, LANES), 1)
    outs = []
    for g in range(AT_KV):
        a0, a1 = accs[g][:tq], accs[g][tq:]
        if g == 0:
            outs.append(jnp.where(lane < AT_D, a0, pltpu.roll(a1, AT_D, 1)))
        else:
            outs.append(jnp.where(lane < AT_D, pltpu.roll(a0, AT_D, 1), a1))
    return jnp.concatenate(outs, axis=1)


def _flash_body(q_ref, k_ref, v0_ref, v1_ref, o_ref, s_ref, *, tq, kc, nkc):
    v_refs = (v0_ref, v1_ref)
    den_row = (AT_D, 0)
    nb = kc // TM
    qs = [jnp.concatenate([q_ref[2 * g], q_ref[2 * g + 1]], axis=1) for g in range(AT_KV)]

    def scores(c, slot):
        k = k_ref[pl.ds(pl.multiple_of(c * kc, kc), kc), :]
        for g in range(AT_KV):
            s_ref[slot, g] = jnp.dot(k, qs[g], preferred_element_type=F32)

    def update(c, slot, carry):
        out = []
        for g, (m, acc) in enumerate(carry):
            s = s_ref[slot, g]
            m_new = jnp.maximum(m, jnp.max(s, axis=0, keepdims=True))
            p = jnp.exp2(s - m_new).astype(BF16)
            pv = jnp.dot(v_refs[g][c * nb], p[0:TM], preferred_element_type=F32)
            for b in range(1, nb):
                pv = pv + jnp.dot(v_refs[g][c * nb + b], p[b * TM:(b + 1) * TM],
                                  preferred_element_type=F32)
            out.append((m_new, jnp.exp2(m - m_new) * acc + pv))
        return tuple(out)

    def pair(i, carry):
        scores(2 * i + 1, 1)
        carry = update(2 * i, 0, carry)
        scores(2 * i + 2, 0)
        return update(2 * i + 1, 1, carry)

    scores(0, 0)
    carry = tuple((jnp.full((1, 2 * tq), -jnp.inf, F32), jnp.zeros((LANES, 2 * tq), F32))
                  for _ in range(AT_KV))
    carry = lax.fori_loop(0, (nkc - 1) // 2, pair, carry)
    if (nkc - 1) % 2:
        scores(nkc - 1, 1)
        carry = update(nkc - 2, 0, carry)
        carry = update(nkc - 1, 1, carry)
    else:
        carry = update(nkc - 1, 0, carry)
    accs = [(acc / acc[den_row[g]:den_row[g] + 1, :]).T for g, (_, acc) in enumerate(carry)]
    o_ref[...] = _assemble_heads(accs, tq)


def _flash_attention(qt, k, v0t, v1t, nq_rows, tq, kv_rows, q_blk0=0):
    nd = kv_rows // TM
    kc = TM * max(d for d in range(1, 6) if nd % d == 0)
    kspec = pl.BlockSpec((kv_rows, LANES), lambda i: (0, 0))
    vspec = pl.BlockSpec((nd, LANES, TM), lambda i: (0, 0, 0))
    return pl.pallas_call(
        functools.partial(_flash_body, tq=tq, kc=kc, nkc=kv_rows // kc),
        grid=(nq_rows // tq,),
        in_specs=[pl.BlockSpec((AT_H, LANES, tq), lambda i: (0, 0, i + q_blk0)),
                  kspec, vspec, vspec],
        out_specs=pl.BlockSpec((tq, AT_W), lambda i: (i, 0)),
        out_shape=jax.ShapeDtypeStruct((nq_rows, AT_W), F32),
        scratch_shapes=[pltpu.VMEM((2, AT_KV, kc, 2 * tq), F32)],
        compiler_params=_cparams(("arbitrary",), V7X_VMEM_LIMIT),
        name="flash_attention",
    )(qt, k, v0t, v1t)


def _layernorm(x, g, b):
    mu = jnp.mean(x, axis=-1, keepdims=True)
    xc = x - mu
    var = jnp.mean(xc * xc, axis=-1, keepdims=True)
    return xc * lax.rsqrt(var + EPS) * g + b


def _outproj_body(*refs, with_ctx, nlat_tiles, alpha):
    if with_ctx:
        (hyl_ref, hyc_ref, yf_ref, yb_ref, z_ref, atl_ref, atc_ref, x_ref, g1_ref, sc2_ref, sh2_ref,
         w_ref, nw_ref, lg_ref, lb_ref, x1_ref, h2_ref) = refs
    else:
        (hyl_ref, yf_ref, yb_ref, z_ref, atl_ref, x_ref, g1_ref, sc2_ref, sh2_ref,
         w_ref, nw_ref, lg_ref, lb_ref, x1_ref, h2_ref) = refs
    hy, at = hyl_ref[...], atl_ref[...]
    if with_ctx:
        is_ctx = pl.program_id(0) == nlat_tiles
        hy = jnp.where(is_ctx, hyc_ref[...], hy)
        at = jnp.where(is_ctx, atc_ref[...], at)
    m = _dot(hy.T, w_ref[0:HY_W, :])
    gated = (yf_ref[...] + yb_ref[...]) * _silu(z_ref[...])
    gw = SSD_W // SSD_G
    parts = []
    for g in range(SSD_G):
        gg = gated[:, g * gw:(g + 1) * gw]
        parts.append(gg * lax.rsqrt(jnp.mean(gg * gg, axis=-1, keepdims=True) + EPS))
    ssd = jnp.concatenate(parts, axis=1) * nw_ref[...]
    m = m + _dot(ssd, w_ref[HY_W:HY_W + SSD_W, :]) + _dot(at, w_ref[HY_W + SSD_W:, :])
    x1 = _layernorm(alpha * x_ref[...] + g1_ref[0] * m, lg_ref[...], lb_ref[...])
    x1_ref[...] = x1
    h2_ref[...] = (x1 * (1.0 + sc2_ref[0]) + sh2_ref[0]).astype(BF16)


def _out_proj(hy_lat, hy_ctx, yf, yb, z, at_lat, at_ctx, x_tok, g1, sc2, sh2, w_out, nw, lg, lb,
              *, with_ctx, alpha):
    t = x_tok.shape[0]
    nlat_tiles = hy_lat.shape[1] // TM
    nt = t // TM if with_ctx else nlat_tiles
    last = t // TM - 1
    sel = lambda i: (jnp.where(i == last, 1, 0), 0, 0)
    row = lambda i: (i, 0)
    lat_row = lambda i: (jnp.minimum(i, nlat_tiles - 1), 0)
    lat_col = lambda i: (0, jnp.minimum(i, nlat_tiles - 1))
    vec = pl.BlockSpec((1, 1, D_MODEL), sel)
    specs = [pl.BlockSpec((HY_W, TM), lat_col)]
    args = [hy_lat]
    if with_ctx:
        specs.append(_full((HY_W, TM)))
        args.append(hy_ctx)
    specs += [pl.BlockSpec((TM, SSD_W), row)] * 3 + [pl.BlockSpec((TM, AT_W), lat_row)]
    args += [yf, yb, z, at_lat]
    if with_ctx:
        specs.append(_full((TM, AT_W)))
        args.append(at_ctx)
    specs += [pl.BlockSpec((TM, D_MODEL), row), vec, vec, vec, _full((MIX_W, D_MODEL)),
              _full((1, SSD_W)), _full((1, D_MODEL)), _full((1, D_MODEL))]
    args += [x_tok, g1, sc2, sh2, w_out, nw, lg, lb]
    return pl.pallas_call(
        functools.partial(_outproj_body, with_ctx=with_ctx, nlat_tiles=nlat_tiles, alpha=alpha),
        grid=(nt,),
        in_specs=specs,
        out_specs=[pl.BlockSpec((TM, D_MODEL), row), pl.BlockSpec((TM, D_MODEL), row)],
        out_shape=[jax.ShapeDtypeStruct((nt * TM, D_MODEL), F32),
                   jax.ShapeDtypeStruct((nt * TM, D_MODEL), BF16)],
        compiler_params=_cparams(("arbitrary",), V7X_VMEM_LIMIT),
        name="out_proj",
    )(*args)


HALO = 16


def _ffn_body(h_ref, hp_ref, hn_ref, x1_ref, g2_ref, wup_ref, cw_ref, cb_ref, wdn_ref, lg_ref, lb_ref,
              o_ref, *, nlat_tiles, ntiles, alpha):
    i = pl.program_id(0)
    pv = ((i != 0) & (i != nlat_tiles)).astype(F32)
    nv = ((i != nlat_tiles - 1) & (i != ntiles - 1)).astype(F32)
    hext = jnp.concatenate([hp_ref[...], h_ref[...], hn_ref[...]], axis=0)
    rows = lax.broadcasted_iota(jnp.int32, (TM + 2 * HALO, 1), 0)
    keep = jnp.where(rows < HALO, pv, jnp.where(rows >= TM + HALO, nv, 1.0))
    acc = jnp.zeros((TM, D_MODEL), F32)
    nchunks = D_FF // FF_CHUNK

    def up(c):
        return [jnp.dot(hext, wup_ref[:, o:o + FF_CHUNK], preferred_element_type=F32)
                for o in (c * FF_CHUNK, D_FF + c * FF_CHUNK)]

    def conv(u, o):
        u = u * keep
        cw = cw_ref[:, o:o + FF_CHUNK]
        prev = pltpu.roll(u, 1, 0)[HALO:HALO + TM]
        nxt = pltpu.roll(u, TM + 2 * HALO - 1, 0)[HALO:HALO + TM]
        return (cw[0:1] * prev + cw[1:2] * u[HALO:HALO + TM] + cw[2:3] * nxt
                + cb_ref[:, o:o + FF_CHUNK])

    ahead = 2
    queue = [up(c) for c in range(ahead)]
    for c in range(nchunks):
        if c + ahead < nchunks:
            queue.append(up(c + ahead))
        us = queue.pop(0)
        c0 = c * FF_CHUNK
        act = (_silu(conv(us[0], c0)) * conv(us[1], D_FF + c0)).astype(BF16)
        acc = acc + jnp.dot(act, wdn_ref[c0:c0 + FF_CHUNK, :], preferred_element_type=F32)
    o_ref[...] = _layernorm(alpha * x1_ref[...] + g2_ref[0] * acc, lg_ref[...], lb_ref[...])


def _conv_ffn(h2, x1, g2, w_up, conv_w, conv_b, w_down, lg, lb, *, nlat_tiles, ntiles, alpha):
    hb = TM // HALO
    lastb = ntiles * hb - 1
    row = lambda i: (i, 0)
    sel = lambda i: (jnp.where(i == nlat_tiles, 1, 0), 0, 0)
    return pl.pallas_call(
        functools.partial(_ffn_body, nlat_tiles=nlat_tiles, ntiles=ntiles, alpha=alpha),
        grid=(ntiles,),
        in_specs=[pl.BlockSpec((TM, D_MODEL), row),
                  pl.BlockSpec((HALO, D_MODEL), lambda i: (jnp.maximum(i * hb - 1, 0), 0)),
                  pl.BlockSpec((HALO, D_MODEL), lambda i: (jnp.minimum((i + 1) * hb, lastb), 0)),
                  pl.BlockSpec((TM, D_MODEL), row),
                  pl.BlockSpec((1, 1, D_MODEL), sel),
                  _full((D_MODEL, 2 * D_FF)), _full((3, 2 * D_FF)), _full((1, 2 * D_FF)),
                  _full((D_FF, D_MODEL)), _full((1, D_MODEL)), _full((1, D_MODEL))],
        out_specs=pl.BlockSpec((TM, D_MODEL), row),
        out_shape=jax.ShapeDtypeStruct((ntiles * TM, D_MODEL), F32),
        compiler_params=_cparams(("arbitrary",), V7X_VMEM_LIMIT),
        name="conv_ffn",
    )(h2, h2, h2, x1, g2, w_up, conv_w, conv_b, w_down, lg, lb)


def _rope_tables(nlat):
    rows = nlat // GRID_W
    row = jnp.repeat(jnp.arange(rows, dtype=F32), GRID_W)
    col = jnp.tile(jnp.arange(GRID_W, dtype=F32), rows)
    nf = AT_D // 4
    inv = ROPE_BASE ** (-jnp.arange(nf, dtype=F32) / nf)
    ang = jnp.concatenate([row[:, None] * inv, col[:, None] * inv], axis=-1)
    cos, sin = jnp.cos(ang), jnp.sin(ang)
    cos_h = jnp.concatenate([cos, cos], axis=1)
    sin_h = jnp.concatenate([-sin, sin], axis=1)
    cos_t = jnp.concatenate([jnp.tile(cos_h, (1, LANES // AT_D)), jnp.ones((CTX, LANES), F32)], axis=0)
    sin_t = jnp.concatenate([jnp.tile(sin_h, (1, LANES // AT_D)), jnp.zeros((CTX, LANES), F32)], axis=0)
    return cos_t, sin_t


def _expansion():
    w = SSD_R * SSD_P
    ex = np.zeros((2 * SSD_G, 4 * LANES, 2 * w), np.float32)
    for d in range(2):
        for g in range(SSD_G):
            e = np.zeros((LANES, w), np.float32)
            for r in range(SSD_R):
                e[d * SSD_H + g * SSD_R + r, r * SSD_P:(r + 1) * SSD_P] = 1.0
            for piece in range(4):
                ex[d * SSD_G + g, piece * LANES:(piece + 1) * LANES,
                   (piece // 2) * w:(piece // 2 + 1) * w] = e
    return jnp.asarray(ex, BF16)


def _group_mean_matrix():
    w = AT_W + AT_KVW
    idx = np.arange(w) // AT_D
    g = (idx[:, None] == idx[None, :]).astype(np.float32) / AT_D
    return jnp.asarray(np.concatenate([g, g], axis=0), BF16)


def _pad_lanes(v, width=LANES):
    return jnp.pad(v, ((0, 0), (0, width - v.shape[1])))


def kernel(x, c, ctx, c_ctx, w_mod, b_mod, w_in, hy_conv_w, hy_conv_b, hy_ffn_w1, hy_ffn_b1, hy_ffn_w2, hy_ffn_b2, hy_ffn_w3, hy_freq, hy_bias, ssd_conv_w, ssd_conv_b, ssd_dt_bias, ssd_a_log, ssd_d, ssd_norm_w, attn_q_norm, attn_k_norm, w_out, ln1_g, ln1_b, ffn_w_up, ffn_conv_w, ffn_conv_b, ffn_w_down, ln2_g, ln2_b):
    nlat = x.shape[1]
    assert x.shape[0] == 1 and ctx.shape[1] == CTX and nlat % (N2 * 8) == 0
    t = nlat + CTX
    nlt = nlat // TM
    alpha = (2.0 * DEPTH) ** 0.25
    cb = 16

    x_tok = jnp.concatenate([x[0], ctx[0]], axis=0)
    cvec = jnp.zeros((8, D_MODEL), F32).at[0].set(c[0]).at[1].set(c_ctx)
    mod = _modulation(cvec, w_mod, b_mod)
    cos_t, sin_t = _rope_tables(nlat)
    ex = _expansion()
    gsum = _group_mean_matrix()
    tb = _fft_tables(nlat)

    for i in range(DEPTH):
        ctx_out = i < DEPTH - 1
        mv = mod[i, 0:2].reshape(2, 6, 1, D_MODEL)
        sh1, sc1, g1, sh2, sc2, g2 = (mv[:, q] for q in range(6))

        wi = w_in[i]
        o = HY_COLS
        w_z = wi[:, o:o + SSD_W]
        w_xbc = wi[:, o + SSD_W:o + SSD_W + SSD_XBC]
        w_dt = _pad_lanes(wi[:, o + SSD_W + SSD_XBC:o + SSD_W + SSD_XBC + 2 * SSD_H])
        w_at = wi[:, o + SSD_W + SSD_XBC + 2 * SSD_H:]
        w_main = jnp.concatenate([w_z, w_xbc, w_dt, w_at], axis=1).astype(BF16)
        w_hyT = wi[:, 0:HY_COLS].T.astype(BF16)

        hyT, z, xbc, dt, qkv = _in_proj(x_tok, sc1, sh1, w_main, w_hyT)

        w1 = hy_ffn_w1[i]
        w1t = _pad_lanes(jnp.concatenate([w1[1:], w1[0:1]], axis=0).T)
        fargs = (w1t, hy_ffn_b1[i][:, None], hy_ffn_w2[i].T, hy_ffn_b2[i][:, None],
                 hy_ffn_w3[i].T, hy_freq[i][:, None])
        taps2d = jnp.concatenate([hy_conv_w[i].T, hy_conv_b[i][:, None]], axis=1)
        taps = taps2d.reshape(-1)
        skip = hy_bias[i].reshape(-1)
        filt = _hyena_filters(nlat, *fargs).reshape(4, HY_W, tb["r1"], N2)
        kf = _filter_spectra(filt, tb, cb)
        hy3 = hyT.reshape(HY_COLS, t // N2, N2)
        zz = _hyena_conv(taps, skip, hy3, 0, hy3, HY_W // cb, kf, tb, cb,
                         conv_u=True, u_off=0, g_off=HY_W, order=0)
        y_hy = _hyena_conv(taps, skip, zz, 0, hy3, 2 * HY_W // cb, kf, tb, cb,
                           conv_u=False, u_off=0, g_off=2 * HY_W, order=1)
        y_hy = y_hy.reshape(HY_W, nlat)
        if ctx_out:
            filt_c = _hyena_filters(CTX, *fargs)
            y_hy_c = _hyena_ctx(hyT, nlat // CTX, filt_c, taps2d, skip[:, None])
        else:
            y_hy_c = None

        d_exp = jnp.repeat(ssd_d[i], SSD_P)[None, :]
        yf, yb = _ssd(xbc, dt, ssd_conv_w[i], ssd_conv_b[i][None, :],
                      _pad_lanes(ssd_dt_bias[i].reshape(1, -1)), _pad_lanes(ssd_a_log[i].reshape(1, -1)),
                      d_exp, ex, nlat // CHUNK)

        qz, kk, v0, v1 = _attn_prep(qkv, cos_t, sin_t, jnp.tile(attn_q_norm[i], AT_H)[None, :],
                                    jnp.tile(attn_k_norm[i], AT_KV)[None, :], gsum)
        y_at = _flash_attention(qz, kk, v0, v1, nlat, 256, t)
        if ctx_out:
            y_at_c = _flash_attention(qz[:, :, nlat:], kk[nlat:], v0[nlt:], v1[nlt:], CTX, 128, CTX)
        else:
            y_at_c = None

        x1, h2 = _out_proj(y_hy, y_hy_c, yf, yb, z, y_at, y_at_c, x_tok, g1, sc2, sh2,
                           w_out[i].astype(BF16), ssd_norm_w[i][None, :], ln1_g[i][None, :],
                           ln1_b[i][None, :], with_ctx=ctx_out, alpha=alpha)
        ntiles = nlt + 1 if ctx_out else nlt
        x_tok = _conv_ffn(h2, x1, g2, ffn_w_up[i].astype(BF16), ffn_conv_w[i], ffn_conv_b[i][None, :],
                          ffn_w_down[i].astype(BF16), ln2_g[i][None, :], ln2_b[i][None, :],
                          nlat_tiles=nlt, ntiles=ntiles, alpha=alpha)
    return x_tok[None]
```

```python
import functools
import math

import numpy as np
import jax
import jax.numpy as jnp
from jax import lax
from jax.experimental import pallas as pl
from jax.experimental.pallas import tpu as pltpu

F32 = jnp.float32
BF16 = jnp.bfloat16
HI = lax.Precision.HIGHEST

D_MODEL = 1024
DEPTH = 2
GRID_W = 64
CTX = 256
EPS = 1e-6

HY_W = 256
HY_BANDS = 16
HY_EMB = 1 + 2 * HY_BANDS
HY_HID = 64
HY_MIN_DECAY = math.log(1e-2) / 1.5
HY_MAX_DECAY = math.log(1e-2) / 0.3
HY_COLS = 3 * HY_W

SSD_H = 8
SSD_P = 64
SSD_W = SSD_H * SSD_P
SSD_G = 2
SSD_R = SSD_H // SSD_G
SSD_N = 128
SSD_XBC = SSD_W + 2 * SSD_G * SSD_N
CHUNK = 128

AT_H = 4
AT_KV = 2
AT_D = 64
AT_W = AT_H * AT_D
AT_KVW = AT_KV * AT_D
ROPE_BASE = 10000.0

MIX_W = HY_W + SSD_W + AT_W
D_FF = 2816
FF_CHUNK = 256

TM = CTX
LANES = 128
VT_ROWS = AT_D + 16
N2 = 256
V7X_VMEM_LIMIT = 56 * 1024 * 1024


def _cparams(sem, vmem=None):
    return pltpu.CompilerParams(dimension_semantics=sem, vmem_limit_bytes=vmem)


def _dot(a, b):
    return jnp.dot(a.astype(BF16), b.astype(BF16), preferred_element_type=F32)


def _dot_hi(a, b):
    return jnp.dot(a, b, preferred_element_type=F32, precision=HI)


def _split(x, n):
    pieces = []
    for _ in range(n):
        p = x.astype(BF16)
        pieces.append(p)
        x = x - p.astype(F32)
    return pieces


def _sigmoid(x):
    return 1.0 / (1.0 + jnp.exp(-x))


def _silu(x):
    return x * _sigmoid(x)


def _full(shape):
    nd = len(shape)
    return pl.BlockSpec(shape, lambda *_: (0,) * nd)


def _mod_body(c_ref, w_ref, b_ref, o_ref):
    s = _silu(c_ref[...])
    o_ref[0] = _dot(s, w_ref[0]) + b_ref[0]


def _modulation(cvec, w_mod, b_mod):
    depth, d, n = w_mod.shape
    nb = 1536
    return pl.pallas_call(
        _mod_body,
        grid=(depth, n // nb),
        in_specs=[pl.BlockSpec((8, d), lambda l, j: (0, 0)),
                  pl.BlockSpec((1, d, nb), lambda l, j: (l, 0, j)),
                  pl.BlockSpec((1, 1, nb), lambda l, j: (l, 0, j))],
        out_specs=pl.BlockSpec((1, 8, nb), lambda l, j: (l, 0, j)),
        out_shape=jax.ShapeDtypeStruct((depth, 8, n), F32),
        compiler_params=_cparams(("arbitrary", "arbitrary")),
        name="modulation",
    )(cvec, w_mod, b_mod.reshape(depth, 1, n))


W_MAIN = SSD_W + SSD_XBC + LANES + AT_W + 2 * AT_KVW


def _inproj_body(*refs, split_x, nlat_tiles):
    if split_x:
        x_ref, xc_ref, *refs = refs
        x = jnp.where(pl.program_id(0) == nlat_tiles, xc_ref[...], x_ref[...])
    else:
        x_ref, *refs = refs
        x = x_ref[...]
    sc_ref, sh_ref, wm_ref, wh_ref, hy_ref, z_ref, xbc_ref, dt_ref, qkv_ref = refs
    h = (x * (1.0 + sc_ref[0]) + sh_ref[0]).astype(BF16)
    main = jnp.dot(h, wm_ref[...], preferred_element_type=F32)
    z_ref[...] = main[:, 0:SSD_W]
    xbc_ref[...] = main[:, SSD_W:SSD_W + SSD_XBC]
    o = SSD_W + SSD_XBC
    dt_ref[...] = main[:, o:o + LANES]
    qkv_ref[...] = main[:, o + LANES:]
    hy_ref[...] = lax.dot_general(wh_ref[...], h, (((1,), (1,)), ((), ())),
                                  preferred_element_type=F32)


def _in_proj(x_main, x_ctx, t, sc, sh, w_main, w_hyT):
    nt = t // TM
    sel = lambda i: (jnp.where(i == nt - 1, 1, 0), 0, 0)
    row = lambda i: (i, 0)
    split_x = x_ctx is not None
    x_specs = [pl.BlockSpec((TM, D_MODEL), lambda i: (jnp.minimum(i, nt - 2), 0)),
               _full((TM, D_MODEL))] if split_x else [pl.BlockSpec((TM, D_MODEL), row)]
    x_args = [x_main, x_ctx] if split_x else [x_main]
    return pl.pallas_call(
        functools.partial(_inproj_body, split_x=split_x, nlat_tiles=nt - 1),
        grid=(nt,),
        in_specs=x_specs + [
                  pl.BlockSpec((1, 1, D_MODEL), sel),
                  pl.BlockSpec((1, 1, D_MODEL), sel),
                  _full((D_MODEL, W_MAIN)),
                  _full((HY_COLS, D_MODEL))],
        out_specs=[pl.BlockSpec((HY_COLS, TM), lambda i: (0, i)),
                   pl.BlockSpec((TM, SSD_W), row),
                   pl.BlockSpec((TM, SSD_XBC), row),
                   pl.BlockSpec((TM, LANES), row),
                   pl.BlockSpec((TM, AT_W + 2 * AT_KVW), row)],
        out_shape=[jax.ShapeDtypeStruct((HY_COLS, t), F32),
                   jax.ShapeDtypeStruct((t, SSD_W), F32),
                   jax.ShapeDtypeStruct((t, SSD_XBC), F32),
                   jax.ShapeDtypeStruct((t, LANES), F32),
                   jax.ShapeDtypeStruct((t, AT_W + 2 * AT_KVW), F32)],
        compiler_params=_cparams(("arbitrary",), V7X_VMEM_LIMIT),
        name="in_proj",
    )(*x_args, sc, sh, w_main, w_hyT)


def _filter_body(w1_ref, b1_ref, w2_ref, b2_ref, w3_ref, fr_ref, o_ref, *, n, nt):
    j = pl.program_id(0)
    k = (lax.broadcasted_iota(jnp.int32, (1, nt), 1) + j * nt).astype(F32)
    t = k * (1.0 / (n - 1))
    wk = k * (2.0 * math.pi / n)
    band = lax.broadcasted_iota(jnp.int32, (HY_BANDS, nt), 0).astype(F32)
    arg = (1e-4 + band * ((HY_BANDS - 1 - 1e-4) / (HY_BANDS - 1))) * wk
    r8 = lax.broadcasted_iota(jnp.int32, (8, nt), 0)
    feats = jnp.concatenate([jnp.cos(arg), -jnp.sin(arg), jnp.where(r8 == 0, t, 0.0),
                             jnp.zeros((LANES - 2 * HY_BANDS - 8, nt), F32)], axis=0)
    fr = fr_ref[...]
    h = jnp.sin(fr * (_dot_hi(w1_ref[...], feats) + b1_ref[...]))
    h = jnp.sin(fr * (_dot_hi(w2_ref[...], h) + b2_ref[...]))
    h = _dot(w3_ref[...], h)
    c = lax.broadcasted_iota(jnp.int32, (HY_W, nt), 0).astype(F32)
    delta = jnp.abs(HY_MIN_DECAY + c * ((HY_MAX_DECAY - HY_MIN_DECAY) / (HY_W - 1)))
    window = jnp.exp(-t * delta)
    for q in range(4):
        o_ref[q] = h[q * HY_W:(q + 1) * HY_W] * window


def _hyena_filters(n, w1t, b1, w2t, b2, w3t, freq):
    nt = min(n, 2048)
    return pl.pallas_call(
        functools.partial(_filter_body, n=n, nt=nt),
        grid=(n // nt,),
        in_specs=[_full((HY_HID, LANES)), _full((HY_HID, 1)), _full((HY_HID, HY_HID)),
                  _full((HY_HID, 1)), _full((4 * HY_W, HY_HID)), _full((HY_HID, 1))],
        out_specs=pl.BlockSpec((4, HY_W, nt), lambda j: (0, 0, j)),
        out_shape=jax.ShapeDtypeStruct((4, HY_W, n), F32),
        compiler_params=_cparams(("arbitrary",)),
        name="hyena_filters",
    )(w1t, b1, w2t, b2, w3t, freq)


def _fft_tables(n):
    big = 2 * n
    n1 = big // N2
    r1 = n1 // 2
    kp = -(-(r1 + 1) // 8) * 8
    k1 = np.arange(kp)[:, None].astype(np.float64)
    live = (k1 <= r1)
    a1 = 2 * np.pi * k1 * np.arange(r1)[None, :] / n1
    f1a = np.concatenate([np.cos(a1) * live, -np.sin(a1) * live], axis=0)
    at = 2 * np.pi * k1 * np.arange(N2)[None, :] / big
    twr, twi = np.cos(at) * live, -np.sin(at) * live
    a2 = 2 * np.pi * np.outer(np.arange(N2), np.arange(N2)) / N2
    c2, s2 = np.cos(a2), np.sin(a2)
    w2f = np.block([[c2, -s2], [s2, c2]])
    w2i = np.block([[c2, s2], [-s2, c2]])
    wgt = np.where((k1 == 0) | (k1 == r1), 1.0, 2.0) * live / big
    f1i = np.concatenate([(np.cos(a1) * wgt).T, (-np.sin(a1) * wgt).T], axis=1)
    j = jnp.asarray
    return dict(r1=r1, kp=kp, f1a=j(f1a, BF16), twr=j(twr, F32), twi=j(twi, F32),
                w2f=j(w2f, BF16), w2i=j(w2i, BF16), f1i=j(f1i, BF16))


def _fft_rows(x, f1a_ref, twr_ref, twi_ref, kp):
    a = jnp.dot(f1a_ref[...], x.astype(BF16), preferred_element_type=F32)
    ar, ai = a[:kp], a[kp:]
    twr, twi = twr_ref[...], twi_ref[...]
    return jnp.concatenate([ar * twr - ai * twi, ar * twi + ai * twr], axis=1)


def _kspec_body(hf_ref, hb_ref, f1a_ref, twr_ref, twi_ref, w2f_ref, o_ref, sa_ref, *, cb, kp, r1):
    row = lax.broadcasted_iota(jnp.int32, (r1, N2), 0)
    lane = lax.broadcasted_iota(jnp.int32, (r1, N2), 1)
    first = (row == 0) & (lane == 0)

    for c in range(cb):
        af = _fft_rows(hf_ref[0, c], f1a_ref, twr_ref, twi_ref, kp)
        ab = _fft_rows(jnp.where(first, 0.0, hb_ref[0, c]), f1a_ref, twr_ref, twi_ref, kp)
        sa_ref[c * kp:(c + 1) * kp, :] = af + ab
        sa_ref[(cb + c) * kp:(cb + c + 1) * kp, :] = af - ab
    kr = jnp.dot(sa_ref[0:cb * kp, :].astype(BF16), w2f_ref[:, 0:N2], preferred_element_type=F32)
    ki = jnp.dot(sa_ref[cb * kp:, :].astype(BF16), w2f_ref[:, N2:], preferred_element_type=F32)
    o_ref[0] = jnp.concatenate([kr, ki], axis=1).reshape(cb, kp, 2 * N2)


def _filter_spectra(filt, tb, cb):
    r1, kp = tb["r1"], tb["kp"]
    return pl.pallas_call(
        functools.partial(_kspec_body, cb=cb, kp=kp, r1=r1),
        grid=(2, HY_W // cb),
        in_specs=[pl.BlockSpec((1, cb, r1, N2), lambda o, j: (2 * o, j, 0, 0)),
                  pl.BlockSpec((1, cb, r1, N2), lambda o, j: (2 * o + 1, j, 0, 0)),
                  _full((2 * kp, r1)), _full((kp, N2)), _full((kp, N2)), _full((2 * N2, 2 * N2))],
        out_specs=pl.BlockSpec((1, cb, kp, 2 * N2), lambda o, j: (o, j, 0, 0)),
        out_shape=jax.ShapeDtypeStruct((2, HY_W, kp, 2 * N2), F32),
        scratch_shapes=[pltpu.VMEM((2 * cb * kp, 2 * N2), F32)],
        compiler_params=_cparams(("arbitrary", "arbitrary"), V7X_VMEM_LIMIT),
        name="hyena_filter_spectra",
    )(filt, filt, tb["f1a"], tb["twr"], tb["twi"], tb["w2f"])


def _short_conv(x, w0, w1, w2, b):
    r1 = x.shape[0]
    row = lax.broadcasted_iota(jnp.int32, x.shape, 0)
    lane = lax.broadcasted_iota(jnp.int32, x.shape, 1)
    a = pltpu.roll(x, 1, 1)
    prev = jnp.where(lane == 0, jnp.where(row == 0, 0.0, pltpu.roll(a, 1, 0)), a)
    a = pltpu.roll(x, N2 - 1, 1)
    nxt = jnp.where(lane == N2 - 1, jnp.where(row == r1 - 1, 0.0, pltpu.roll(a, r1 - 1, 0)), a)
    return w0 * prev + w1 * x + w2 * nxt + b


def _hyconv_body(taps_ref, skip_ref, u_ref, g_ref, kf_ref, f1a_ref, twr_ref, twi_ref, w2f_ref,
                 w2i_ref, f1i_ref, o_ref, sa_ref, sq_ref, sx_ref, *, cb, kp, conv_u, u_off,
                 g_off, order):
    j = pl.program_id(0)

    def taps(ch):
        return (taps_ref[ch * 4], taps_ref[ch * 4 + 1], taps_ref[ch * 4 + 2], taps_ref[ch * 4 + 3])

    for c in range(cb):
        x = u_ref[c]
        if conv_u:
            x = _short_conv(x, *taps(u_off + j * cb + c))
        sx_ref[c] = x
        sa_ref[c * kp:(c + 1) * kp, :] = _fft_rows(x, f1a_ref, twr_ref, twi_ref, kp)
    x = jnp.dot(sa_ref[...].astype(BF16), w2f_ref[...], preferred_element_type=F32)
    k = kf_ref[0].reshape(cb * kp, 2 * N2)
    xr, xi, kr, ki = x[:, :N2], x[:, N2:], k[:, :N2], k[:, N2:]
    p = jnp.concatenate([xr * kr - xi * ki, xr * ki + xi * kr], axis=1).astype(BF16)
    sq_ref[...] = jnp.dot(p, w2i_ref[...], preferred_element_type=F32)

    for c in range(cb):
        q = sq_ref[c * kp:(c + 1) * kp, :]
        qr, qi = q[:, :N2], q[:, N2:]
        twr, twi = twr_ref[...], twi_ref[...]
        y2 = jnp.concatenate([qr * twr + qi * twi, qi * twr - qr * twi], axis=0).astype(BF16)
        y = jnp.dot(f1i_ref[...], y2, preferred_element_type=F32)
        y = y + skip_ref[order * HY_W + j * cb + c] * sx_ref[c]
        o_ref[c] = _short_conv(g_ref[c], *taps(g_off + j * cb + c)) * y


def _hyena_conv(taps, skip, u3, u_blk0, g3, g_blk0, kf, tb, cb, *, conv_u, u_off, g_off, order):
    r1, kp = tb["r1"], tb["kp"]
    smem = pl.BlockSpec(memory_space=pltpu.SMEM)
    return pl.pallas_call(
        functools.partial(_hyconv_body, cb=cb, kp=kp, conv_u=conv_u, u_off=u_off, g_off=g_off,
                          order=order),
        grid=(HY_W // cb,),
        in_specs=[smem, smem,
                  pl.BlockSpec((cb, r1, N2), lambda j: (j + u_blk0, 0, 0)),
                  pl.BlockSpec((cb, r1, N2), lambda j: (j + g_blk0, 0, 0)),
                  pl.BlockSpec((1, cb, kp, 2 * N2), lambda j: (order, j, 0, 0)),
                  _full((2 * kp, r1)), _full((kp, N2)), _full((kp, N2)),
                  _full((2 * N2, 2 * N2)), _full((2 * N2, 2 * N2)), _full((r1, 2 * kp))],
        out_specs=pl.BlockSpec((cb, r1, N2), lambda j: (j, 0, 0)),
        out_shape=jax.ShapeDtypeStruct((HY_W, r1, N2), F32),
        scratch_shapes=[pltpu.VMEM((cb * kp, 2 * N2), F32),
                        pltpu.VMEM((cb * kp, 2 * N2), F32),
                        pltpu.VMEM((cb, r1, N2), F32)],
        compiler_params=_cparams(("arbitrary",), V7X_VMEM_LIMIT),
        name="hyena_conv%d" % order,
    )(taps, skip, u3, g3, kf, tb["f1a"], tb["twr"], tb["twi"], tb["w2f"], tb["w2i"], tb["f1i"])


def _ctx_tables(n):
    big = 2 * n
    a = 2 * np.pi * np.outer(np.arange(n), np.arange(big)) / big
    fc = np.concatenate([np.cos(a), -np.sin(a)], axis=1)
    gi = np.concatenate([np.cos(a).T, -np.sin(a).T], axis=0) / big
    return jnp.asarray(fc, BF16), jnp.asarray(gi, BF16)


def _hyctx_body(hy_ref, filt_ref, taps_ref, skip_ref, fc_ref, gi_ref, o_ref, *, n):
    big = 2 * n
    lane = lax.broadcasted_iota(jnp.int32, (HY_W, n), 1)

    def sconv(x, t):
        prev = jnp.where(lane == 0, 0.0, pltpu.roll(x, 1, 1))
        nxt = jnp.where(lane == n - 1, 0.0, pltpu.roll(x, n - 1, 1))
        return t[:, 0:1] * prev + t[:, 1:2] * x + t[:, 2:3] * nxt + t[:, 3:4]

    def spec(o):
        xf = _dot(filt_ref[2 * o], fc_ref[...])
        xb = _dot(jnp.where(lane == 0, 0.0, filt_ref[2 * o + 1]), fc_ref[...])
        return xf[:, :big] + xb[:, :big], xf[:, big:] - xb[:, big:]

    def lconv(u, o):
        kr, ki = spec(o)
        x = _dot(u, fc_ref[...])
        xr, xi = x[:, :big], x[:, big:]
        p = jnp.concatenate([xr * kr - xi * ki, xr * ki + xi * kr], axis=1)
        return _dot(p, gi_ref[...]) + skip_ref[o * HY_W:(o + 1) * HY_W, :] * u

    v = sconv(hy_ref[0:HY_W, :], taps_ref[0:HY_W, :])
    x1 = sconv(hy_ref[HY_W:2 * HY_W, :], taps_ref[HY_W:2 * HY_W, :])
    x2 = sconv(hy_ref[2 * HY_W:, :], taps_ref[2 * HY_W:, :])
    z = x1 * lconv(v, 0)
    o_ref[...] = x2 * lconv(z, 1)


def _hyena_ctx(hyT, blk, filt_ctx, taps2d, skip2d):
    n = filt_ctx.shape[-1]
    fc, gi = _ctx_tables(n)
    return pl.pallas_call(
        functools.partial(_hyctx_body, n=n),
        grid=(1,),
        in_specs=[pl.BlockSpec((HY_COLS, n), lambda i: (0, blk)),
                  _full((4, HY_W, n)), _full((HY_COLS, 4)), _full((2 * HY_W, 1)),
                  _full((n, 4 * n)), _full((4 * n, n))],
        out_specs=_full((HY_W, n)),
        out_shape=jax.ShapeDtypeStruct((HY_W, n), F32),
        compiler_params=_cparams(("arbitrary",), V7X_VMEM_LIMIT),
        name="hyena_ctx",
    )(hyT, filt_ctx, taps2d, skip2d, fc, gi)


def _ssd_body(xf_ref, xfp_ref, xfn_ref, dtf_ref, xb_ref, xbp_ref, xbn_ref, dtb_ref,
              cw_ref, cb_ref, dtbias_ref, alog_ref, dexp_ref, ex_ref,
              yf_ref, yb_ref, h_ref, sx_ref, *, nlat):
    j = pl.program_id(0)
    nch = nlat + CTX // CHUNK

    @pl.when(j == 0)
    def _():
        h_ref[...] = jnp.zeros_like(h_ref)

    ri = lax.broadcasted_iota(jnp.int32, (CHUNK, CHUNK), 0)
    ci = lax.broadcasted_iota(jnp.int32, (CHUNK, CHUNK), 1)
    a_all = -jnp.exp(alog_ref[...])

    for d in range(2):
        x_ref, xp_ref, xn_ref, dt_ref, y_ref = (
            (xf_ref, xfp_ref, xfn_ref, dtf_ref, yf_ref) if d == 0
            else (xb_ref, xbp_ref, xbn_ref, dtb_ref, yb_ref))
        cid = jnp.where(j < 2, nlat + j, j - 2) if d == 0 else nch - 1 - j
        pv = ((cid != 0) & (cid != nlat)).astype(F32)
        nv = ((cid != nlat - 1) & (cid != nch - 1)).astype(F32)
        sx_ref[0:8, :] = xp_ref[...] * pv
        sx_ref[8:8 + CHUNK, :] = x_ref[...]
        sx_ref[8 + CHUNK:, :] = xn_ref[...] * nv
        pre = (cw_ref[0:1, :] * sx_ref[pl.ds(7, CHUNK), :] + cw_ref[1:2, :] * x_ref[...]
               + cw_ref[2:3, :] * sx_ref[pl.ds(9, CHUNK), :] + cb_ref[...])
        xc = _silu(pre)
        xs = xc[:, 0:SSD_W]
        z = dt_ref[...] + dtbias_ref[...]
        dtv = jnp.maximum(z, 0.0) + jnp.log(1.0 + jnp.exp(-jnp.abs(z)))
        da = dtv * a_all
        tri = (ri >= ci) if d == 0 else (ri <= ci)
        r3 = jnp.dot(tri.astype(BF16), jnp.concatenate(_split(da, 3), axis=1),
                     preferred_element_type=F32)
        acum = r3[:, 0:LANES] + r3[:, LANES:2 * LANES] + r3[:, 2 * LANES:]
        last = CHUNK - 1 if d == 0 else 0
        acum_t = acum.T
        pieces = jnp.concatenate(_split(acum, 2) + _split(dtv, 2), axis=1)
        ys = []
        for g in range(SSD_G):
            bm = xc[:, SSD_W + g * SSD_N:SSD_W + (g + 1) * SSD_N]
            cm = xc[:, SSD_W + (SSD_G + g) * SSD_N:SSD_W + (SSD_G + g + 1) * SSD_N]
            bt = bm.T
            scores = _dot(cm, bt)
            both = jnp.dot(pieces, ex_ref[d * SSD_G + g], preferred_element_type=F32)
            acum_e, dt_e = both[:, 0:SSD_R * SSD_P], both[:, SSD_R * SSD_P:]
            tot_e = acum_e[last:last + 1, :]
            xg = xs[:, g * SSD_R * SSD_P:(g + 1) * SSD_R * SSD_P]
            hg = h_ref[d, g]
            yoff = _dot(cm, hg) * jnp.exp(acum_e)
            ydiag = []
            for r in range(SSD_R):
                hl = d * SSD_H + g * SSD_R + r
                seg = acum[:, hl:hl + 1] - acum_t[hl:hl + 1, :]
                m = scores * jnp.exp(jnp.where(tri, seg, -jnp.inf))
                xdt = xg[:, r * SSD_P:(r + 1) * SSD_P] * dt_e[:, r * SSD_P:(r + 1) * SSD_P]
                ydiag.append(_dot(m, xdt))
            ys.append(jnp.concatenate(ydiag, axis=1) + yoff)
            xw = xg * (jnp.exp(tot_e - acum_e) * dt_e)
            h_ref[d, g] = hg * jnp.exp(tot_e) + _dot(bt, xw)
        y = jnp.concatenate(ys, axis=1)
        if d == 0:
            y = y + dexp_ref[...] * xs
        y_ref[...] = y


def _ssd(xbc, dt, conv_w, conv_b, dt_bias, a_log, d_exp, ex, nlat):
    t = xbc.shape[0]
    nch = t // CHUNK
    hb = CHUNK // 8
    last8 = t // 8 - 1
    fc = lambda j: jnp.where(j < 2, nlat + j, j - 2)
    bc = lambda j: nch - 1 - j
    prev = lambda f: (lambda j: (jnp.maximum(f(j) * hb - 1, 0), 0))
    nxt = lambda f: (lambda j: (jnp.minimum((f(j) + 1) * hb, last8), 0))
    main = lambda f: (lambda j: (f(j), 0))
    return pl.pallas_call(
        functools.partial(_ssd_body, nlat=nlat),
        grid=(nch,),
        in_specs=[pl.BlockSpec((CHUNK, SSD_XBC), main(fc)), pl.BlockSpec((8, SSD_XBC), prev(fc)),
                  pl.BlockSpec((8, SSD_XBC), nxt(fc)), pl.BlockSpec((CHUNK, LANES), main(fc)),
                  pl.BlockSpec((CHUNK, SSD_XBC), main(bc)), pl.BlockSpec((8, SSD_XBC), prev(bc)),
                  pl.BlockSpec((8, SSD_XBC), nxt(bc)), pl.BlockSpec((CHUNK, LANES), main(bc)),
                  _full((3, SSD_XBC)), _full((1, SSD_XBC)), _full((1, LANES)), _full((1, LANES)),
                  _full((1, SSD_W)), _full((2 * SSD_G, 4 * LANES, 2 * SSD_R * SSD_P))],
        out_specs=[pl.BlockSpec((CHUNK, SSD_W), main(fc)), pl.BlockSpec((CHUNK, SSD_W), main(bc))],
        out_shape=[jax.ShapeDtypeStruct((t, SSD_W), F32), jax.ShapeDtypeStruct((t, SSD_W), F32)],
        scratch_shapes=[pltpu.VMEM((2, SSD_G, SSD_N, SSD_R * SSD_P), F32),
                        pltpu.VMEM((CHUNK + 16, SSD_XBC), F32)],
        compiler_params=_cparams(("arbitrary",), V7X_VMEM_LIMIT),
        name="ssd_scan",
    )(xbc, xbc, xbc, dt, xbc, xbc, xbc, dt, conv_w, conv_b, dt_bias, a_log, d_exp, ex)


def _attn_prep_body(qkv_ref, cos_ref, sin_ref, qn_ref, kn_ref, gsum_ref, q_ref, k_ref, v0_ref, v1_ref):
    x = qkv_ref[...]
    qk = x[:, 0:AT_W + AT_KVW]
    ms = jnp.dot(jnp.concatenate(_split(qk * qk, 2), axis=1), gsum_ref[...],
                 preferred_element_type=F32)
    wn = jnp.concatenate([qn_ref[...], kn_ref[...]], axis=1)
    qk = qk * lax.rsqrt(ms + EPS) * wn
    w = qk.shape[1]
    lane = lax.broadcasted_iota(jnp.int32, qk.shape, 1)
    partner = jnp.where(lane % AT_D < AT_D // 2, pltpu.roll(qk, w - AT_D // 2, 1),
                        pltpu.roll(qk, AT_D // 2, 1))
    cos = jnp.concatenate([cos_ref[...]] * 3, axis=1)
    sin = jnp.concatenate([sin_ref[...]] * 3, axis=1)
    qk = qk * cos + partner * sin
    q = qk[:, 0:AT_W] * (AT_D ** -0.5 * math.log2(math.e))
    zeros = jnp.zeros((q.shape[0], AT_D), F32)
    for h in range(AT_H):
        qh = q[:, h * AT_D:(h + 1) * AT_D]
        qh = (jnp.concatenate([qh, zeros], axis=1) if h // 2 == 0
              else jnp.concatenate([zeros, qh], axis=1))
        q_ref[h] = qh.T.astype(BF16)
    k_ref[...] = qk[:, AT_W:].astype(BF16)
    v = x[:, AT_W + AT_KVW:]
    vl = lax.broadcasted_iota(jnp.int32, v.shape, 1)
    for g, v_ref in enumerate((v0_ref, v1_ref)):
        vg = v if g == 0 else pltpu.roll(v, AT_D, 1)
        vg = jnp.where(vl < AT_D, vg, jnp.where(vl == AT_D, 1.0, 0.0))
        v_ref[0] = vg.T[0:VT_ROWS].astype(BF16)


def _attn_prep(qkv, cos_t, sin_t, qn, kn, gsum):
    t = qkv.shape[0]
    row = lambda i: (i, 0)
    return pl.pallas_call(
        _attn_prep_body,
        grid=(t // TM,),
        in_specs=[pl.BlockSpec((TM, AT_W + 2 * AT_KVW), row), pl.BlockSpec((TM, LANES), row),
                  pl.BlockSpec((TM, LANES), row), _full((1, AT_W)), _full((1, AT_KVW)),
                  _full((2 * (AT_W + AT_KVW), AT_W + AT_KVW))],
        out_specs=[pl.BlockSpec((AT_H, LANES, TM), lambda i: (0, 0, i)),
                   pl.BlockSpec((TM, LANES), row),
                   pl.BlockSpec((1, VT_ROWS, TM), lambda i: (i, 0, 0)),
                   pl.BlockSpec((1, VT_ROWS, TM), lambda i: (i, 0, 0))],
        out_shape=[jax.ShapeDtypeStruct((AT_H, LANES, t), BF16),
                   jax.ShapeDtypeStruct((t, LANES), BF16),
                   jax.ShapeDtypeStruct((t // TM, VT_ROWS, TM), BF16),
                   jax.ShapeDtypeStruct((t // TM, VT_ROWS, TM), BF16)],
        compiler_params=_cparams(("arbitrary",)),
        name="attn_prep",
    )(qkv, cos_t, sin_t, qn, kn, gsum)


def _flash_body(q_ref, k_ref, v0_ref, v1_ref, o_ref, s_ref, *, tq, kc, nkc):
    v_refs = (v0_ref, v1_ref)
    nb = kc // TM
    qs = [jnp.concatenate([q_ref[2 * g], q_ref[2 * g + 1]], axis=1) for g in range(AT_KV)]

    def scores(c, slot):
        k = k_ref[pl.ds(pl.multiple_of(c * kc, kc), kc), :]
        for g in range(AT_KV):
            s_ref[slot, g] = jnp.dot(k, qs[g], preferred_element_type=F32)

    def update(c, slot, carry):
        out = []
        for g, (m, acc) in enumerate(carry):
            s = s_ref[slot, g]
            m_new = jnp.maximum(m, jnp.max(s, axis=0, keepdims=True))
            p = jnp.exp2(s - m_new).astype(BF16)
            pv = jnp.dot(v_refs[g][c * nb], p[0:TM], preferred_element_type=F32)
            for b in range(1, nb):
                pv = pv + jnp.dot(v_refs[g][c * nb + b], p[b * TM:(b + 1) * TM],
                                  preferred_element_type=F32)
            out.append((m_new, jnp.exp2(m - m_new) * acc + pv))
        return tuple(out)

    def pair(i, carry):
        scores(2 * i + 1, 1)
        carry = update(2 * i, 0, carry)
        scores(2 * i + 2, 0)
        return update(2 * i + 1, 1, carry)

    scores(0, 0)
    carry = tuple((jnp.full((1, 2 * tq), -jnp.inf, F32), jnp.zeros((VT_ROWS, 2 * tq), F32))
                  for _ in range(AT_KV))
    carry = lax.fori_loop(0, (nkc - 1) // 2, pair, carry)
    if (nkc - 1) % 2:
        scores(nkc - 1, 1)
        carry = update(nkc - 2, 0, carry)
        carry = update(nkc - 1, 1, carry)
    else:
        carry = update(nkc - 1, 0, carry)
    lane = lax.broadcasted_iota(jnp.int32, (tq, LANES), 1)
    outs = []
    for _, acc in carry:
        o = acc[0:AT_D] / acc[AT_D:AT_D + 1, :]
        o = jnp.concatenate([o, jnp.zeros((LANES - AT_D, 2 * tq), F32)], axis=0).T
        outs.append(jnp.where(lane < AT_D, o[:tq], pltpu.roll(o[tq:], AT_D, 1)))
    o_ref[...] = jnp.concatenate(outs, axis=1)


def _flash_attention(qt, k, v0t, v1t, nq_rows, tq, kv_rows, q_blk0=0):
    nd = kv_rows // TM
    kc = TM * max(d for d in range(1, 6) if nd % d == 0)
    kspec = pl.BlockSpec((kv_rows, LANES), lambda i: (0, 0))
    vspec = pl.BlockSpec((nd, VT_ROWS, TM), lambda i: (0, 0, 0))
    return pl.pallas_call(
        functools.partial(_flash_body, tq=tq, kc=kc, nkc=kv_rows // kc),
        grid=(nq_rows // tq,),
        in_specs=[pl.BlockSpec((AT_H, LANES, tq), lambda i: (0, 0, i + q_blk0)),
                  kspec, vspec, vspec],
        out_specs=pl.BlockSpec((tq, AT_W), lambda i: (i, 0)),
        out_shape=jax.ShapeDtypeStruct((nq_rows, AT_W), F32),
        scratch_shapes=[pltpu.VMEM((2, AT_KV, kc, 2 * tq), F32)],
        compiler_params=_cparams(("arbitrary",), V7X_VMEM_LIMIT),
        name="flash_attention",
    )(qt, k, v0t, v1t)


def _layernorm(x, g, b):
    mu = jnp.mean(x, axis=-1, keepdims=True)
    xc = x - mu
    var = jnp.mean(xc * xc, axis=-1, keepdims=True)
    return xc * lax.rsqrt(var + EPS) * g + b


def _outproj_body(*refs, with_ctx, split_x, nlat_tiles, alpha):
    refs = list(refs)
    take = lambda n: [refs.pop(0) for _ in range(n)]
    hyl_ref, = take(1)
    hyc_ref, = take(1) if with_ctx else (None,)
    yf_ref, yb_ref, z_ref, atl_ref = take(4)
    atc_ref, = take(1) if with_ctx else (None,)
    x_ref, = take(1)
    xc_ref, = take(1) if split_x else (None,)
    g1_ref, sc2_ref, sh2_ref, w_ref, nw_ref, lg_ref, lb_ref, x1_ref, h2_ref = refs
    hy, at, x = hyl_ref[...], atl_ref[...], x_ref[...]
    if with_ctx:
        is_ctx = pl.program_id(0) == nlat_tiles
        hy = jnp.where(is_ctx, hyc_ref[...], hy)
        at = jnp.where(is_ctx, atc_ref[...], at)
        if split_x:
            x = jnp.where(is_ctx, xc_ref[...], x)
    m = _dot(hy.T, w_ref[0:HY_W, :])
    gated = (yf_ref[...] + yb_ref[...]) * _silu(z_ref[...])
    gw = SSD_W // SSD_G
    parts = []
    for g in range(SSD_G):
        gg = gated[:, g * gw:(g + 1) * gw]
        parts.append(gg * lax.rsqrt(jnp.mean(gg * gg, axis=-1, keepdims=True) + EPS))
    ssd = jnp.concatenate(parts, axis=1) * nw_ref[...]
    m = m + _dot(ssd, w_ref[HY_W:HY_W + SSD_W, :]) + _dot(at, w_ref[HY_W + SSD_W:, :])
    x1 = _layernorm(alpha * x + g1_ref[0] * m, lg_ref[...], lb_ref[...])
    x1_ref[...] = x1
    h2_ref[...] = (x1 * (1.0 + sc2_ref[0]) + sh2_ref[0]).astype(BF16)


def _out_proj(hy_lat, hy_ctx, yf, yb, z, at_lat, at_ctx, x_main, x_ctx, g1, sc2, sh2, w_out, nw, lg, lb,
              *, with_ctx, alpha):
    t = yf.shape[0]
    split_x = with_ctx and x_ctx is not None
    nlat_tiles = hy_lat.shape[1] // TM
    nt = t // TM if with_ctx else nlat_tiles
    last = t // TM - 1
    sel = lambda i: (jnp.where(i == last, 1, 0), 0, 0)
    row = lambda i: (i, 0)
    lat_row = lambda i: (jnp.minimum(i, nlat_tiles - 1), 0)
    lat_col = lambda i: (0, jnp.minimum(i, nlat_tiles - 1))
    vec = pl.BlockSpec((1, 1, D_MODEL), sel)
    specs = [pl.BlockSpec((HY_W, TM), lat_col)]
    args = [hy_lat]
    if with_ctx:
        specs.append(_full((HY_W, TM)))
        args.append(hy_ctx)
    specs += [pl.BlockSpec((TM, SSD_W), row)] * 3 + [pl.BlockSpec((TM, AT_W), lat_row)]
    args += [yf, yb, z, at_lat]
    if with_ctx:
        specs.append(_full((TM, AT_W)))
        args.append(at_ctx)
    specs.append(pl.BlockSpec((TM, D_MODEL), lat_row if x_ctx is not None else row))
    args.append(x_main)
    if split_x:
        specs.append(_full((TM, D_MODEL)))
        args.append(x_ctx)
    specs += [vec, vec, vec, _full((MIX_W, D_MODEL)),
              _full((1, SSD_W)), _full((1, D_MODEL)), _full((1, D_MODEL))]
    args += [g1, sc2, sh2, w_out, nw, lg, lb]
    return pl.pallas_call(
        functools.partial(_outproj_body, with_ctx=with_ctx, split_x=split_x, nlat_tiles=nlat_tiles,
                          alpha=alpha),
        grid=(nt,),
        in_specs=specs,
        out_specs=[pl.BlockSpec((TM, D_MODEL), row), pl.BlockSpec((TM, D_MODEL), row)],
        out_shape=[jax.ShapeDtypeStruct((nt * TM, D_MODEL), F32),
                   jax.ShapeDtypeStruct((nt * TM, D_MODEL), BF16)],
        compiler_params=_cparams(("arbitrary",), V7X_VMEM_LIMIT),
        name="out_proj",
    )(*args)


HALO = 16


def _ffn_body(h_ref, hp_ref, hn_ref, x1_ref, g2_ref, wup_ref, cw_ref, cb_ref, wdn_ref, lg_ref, lb_ref,
              o_ref, *, nlat_tiles, ntiles, alpha):
    i = pl.program_id(0)
    pv = (i != 0) & (i != nlat_tiles)
    nv = (i != nlat_tiles - 1) & (i != ntiles - 1)
    zero = jnp.zeros((HALO, D_MODEL), BF16)
    hp = jnp.where(pv, hp_ref[...], zero)
    hn = jnp.where(nv, hn_ref[...], zero)
    hext = jnp.concatenate([hp, h_ref[...], hn], axis=0)
    acc = jnp.zeros((TM, D_MODEL), F32)
    nchunks = D_FF // FF_CHUNK

    def up(c):
        return [jnp.dot(hext, wup_ref[:, o:o + FF_CHUNK], preferred_element_type=F32)
                for o in (c * FF_CHUNK, D_FF + c * FF_CHUNK)]

    def conv(u, o):
        cw = cw_ref[:, o:o + FF_CHUNK]
        prev = pltpu.roll(u, 1, 0)[HALO:HALO + TM]
        nxt = pltpu.roll(u, TM + 2 * HALO - 1, 0)[HALO:HALO + TM]
        return (cw[0:1] * prev + cw[1:2] * u[HALO:HALO + TM] + cw[2:3] * nxt
                + cb_ref[:, o:o + FF_CHUNK])

    ahead = 2
    queue = [up(c) for c in range(ahead)]
    for c in range(nchunks):
        if c + ahead < nchunks:
            queue.append(up(c + ahead))
        us = queue.pop(0)
        c0 = c * FF_CHUNK
        act = (_silu(conv(us[0], c0)) * conv(us[1], D_FF + c0)).astype(BF16)
        acc = acc + jnp.dot(act, wdn_ref[c0:c0 + FF_CHUNK, :], preferred_element_type=F32)
    o_ref[...] = _layernorm(alpha * x1_ref[...] + g2_ref[0] * acc, lg_ref[...], lb_ref[...])


def _conv_ffn(h2, x1, g2, w_up, conv_w, conv_b, w_down, lg, lb, *, nlat_tiles, ntiles, alpha):
    hb = TM // HALO
    lastb = ntiles * hb - 1
    row = lambda i: (i, 0)
    sel = lambda i: (jnp.where(i == nlat_tiles, 1, 0), 0, 0)
    return pl.pallas_call(
        functools.partial(_ffn_body, nlat_tiles=nlat_tiles, ntiles=ntiles, alpha=alpha),
        grid=(ntiles,),
        in_specs=[pl.BlockSpec((TM, D_MODEL), row),
                  pl.BlockSpec((HALO, D_MODEL), lambda i: (jnp.maximum(i * hb - 1, 0), 0)),
                  pl.BlockSpec((HALO, D_MODEL), lambda i: (jnp.minimum((i + 1) * hb, lastb), 0)),
                  pl.BlockSpec((TM, D_MODEL), row),
                  pl.BlockSpec((1, 1, D_MODEL), sel),
                  _full((D_MODEL, 2 * D_FF)), _full((3, 2 * D_FF)), _full((1, 2 * D_FF)),
                  _full((D_FF, D_MODEL)), _full((1, D_MODEL)), _full((1, D_MODEL))],
        out_specs=pl.BlockSpec((TM, D_MODEL), row),
        out_shape=jax.ShapeDtypeStruct((ntiles * TM, D_MODEL), F32),
        compiler_params=_cparams(("arbitrary",), V7X_VMEM_LIMIT),
        name="conv_ffn",
    )(h2, h2, h2, x1, g2, w_up, conv_w, conv_b, w_down, lg, lb)


def _rope_tables(nlat):
    rows = nlat // GRID_W
    row = jnp.repeat(jnp.arange(rows, dtype=F32), GRID_W)
    col = jnp.tile(jnp.arange(GRID_W, dtype=F32), rows)
    nf = AT_D // 4
    inv = ROPE_BASE ** (-jnp.arange(nf, dtype=F32) / nf)
    ang = jnp.concatenate([row[:, None] * inv, col[:, None] * inv], axis=-1)
    cos, sin = jnp.cos(ang), jnp.sin(ang)
    cos_h = jnp.concatenate([cos, cos], axis=1)
    sin_h = jnp.concatenate([-sin, sin], axis=1)
    cos_t = jnp.concatenate([jnp.tile(cos_h, (1, LANES // AT_D)), jnp.ones((CTX, LANES), F32)], axis=0)
    sin_t = jnp.concatenate([jnp.tile(sin_h, (1, LANES // AT_D)), jnp.zeros((CTX, LANES), F32)], axis=0)
    return cos_t, sin_t


def _expansion():
    w = SSD_R * SSD_P
    ex = np.zeros((2 * SSD_G, 4 * LANES, 2 * w), np.float32)
    for d in range(2):
        for g in range(SSD_G):
            e = np.zeros((LANES, w), np.float32)
            for r in range(SSD_R):
                e[d * SSD_H + g * SSD_R + r, r * SSD_P:(r + 1) * SSD_P] = 1.0
            for piece in range(4):
                ex[d * SSD_G + g, piece * LANES:(piece + 1) * LANES,
                   (piece // 2) * w:(piece // 2 + 1) * w] = e
    return jnp.asarray(ex, BF16)


def _group_mean_matrix():
    w = AT_W + AT_KVW
    idx = np.arange(w) // AT_D
    g = (idx[:, None] == idx[None, :]).astype(np.float32) / AT_D
    return jnp.asarray(np.concatenate([g, g], axis=0), BF16)


def _pad_lanes(v, width=LANES):
    return jnp.pad(v, ((0, 0), (0, width - v.shape[1])))


def kernel(x, c, ctx, c_ctx, w_mod, b_mod, w_in, hy_conv_w, hy_conv_b, hy_ffn_w1, hy_ffn_b1, hy_ffn_w2, hy_ffn_b2, hy_ffn_w3, hy_freq, hy_bias, ssd_conv_w, ssd_conv_b, ssd_dt_bias, ssd_a_log, ssd_d, ssd_norm_w, attn_q_norm, attn_k_norm, w_out, ln1_g, ln1_b, ffn_w_up, ffn_conv_w, ffn_conv_b, ffn_w_down, ln2_g, ln2_b):
    nlat = x.shape[1]
    assert x.shape[0] == 1 and ctx.shape[1] == CTX and nlat % (N2 * 8) == 0
    t = nlat + CTX
    nlt = nlat // TM
    alpha = (2.0 * DEPTH) ** 0.25
    cb = 16

    x_main, x_ctx = x[0], ctx[0]
    cvec = jnp.zeros((8, D_MODEL), F32).at[0].set(c[0]).at[1].set(c_ctx)
    mod = _modulation(cvec, w_mod, b_mod)
    cos_t, sin_t = _rope_tables(nlat)
    ex = _expansion()
    gsum = _group_mean_matrix()
    tb = _fft_tables(nlat)

    for i in range(DEPTH):
        ctx_out = i < DEPTH - 1
        mv = mod[i, 0:2].reshape(2, 6, 1, D_MODEL)
        sh1, sc1, g1, sh2, sc2, g2 = (mv[:, q] for q in range(6))

        wi = w_in[i]
        o = HY_COLS
        w_z = wi[:, o:o + SSD_W]
        w_xbc = wi[:, o + SSD_W:o + SSD_W + SSD_XBC]
        w_dt = _pad_lanes(wi[:, o + SSD_W + SSD_XBC:o + SSD_W + SSD_XBC + 2 * SSD_H])
        w_at = wi[:, o + SSD_W + SSD_XBC + 2 * SSD_H:]
        w_main = jnp.concatenate([w_z, w_xbc, w_dt, w_at], axis=1).astype(BF16)
        w_hyT = wi[:, 0:HY_COLS].T.astype(BF16)

        hyT, z, xbc, dt, qkv = _in_proj(x_main, x_ctx, t, sc1, sh1, w_main, w_hyT)

        w1 = hy_ffn_w1[i]
        w1t = _pad_lanes(jnp.concatenate([w1[1:], w1[0:1]], axis=0).T)
        fargs = (w1t, hy_ffn_b1[i][:, None], hy_ffn_w2[i].T, hy_ffn_b2[i][:, None],
                 hy_ffn_w3[i].T, hy_freq[i][:, None])
        taps2d = jnp.concatenate([hy_conv_w[i].T, hy_conv_b[i][:, None]], axis=1)
        taps = taps2d.reshape(-1)
        skip = hy_bias[i].reshape(-1)
        filt = _hyena_filters(nlat, *fargs).reshape(4, HY_W, tb["r1"], N2)
        kf = _filter_spectra(filt, tb, cb)
        hy3 = hyT.reshape(HY_COLS, t // N2, N2)
        zz = _hyena_conv(taps, skip, hy3, 0, hy3, HY_W // cb, kf, tb, cb,
                         conv_u=True, u_off=0, g_off=HY_W, order=0)
        y_hy = _hyena_conv(taps, skip, zz, 0, hy3, 2 * HY_W // cb, kf, tb, cb,
                           conv_u=False, u_off=0, g_off=2 * HY_W, order=1)
        y_hy = y_hy.reshape(HY_W, nlat)
        if ctx_out:
            filt_c = _hyena_filters(CTX, *fargs)
            y_hy_c = _hyena_ctx(hyT, nlat // CTX, filt_c, taps2d, skip[:, None])
        else:
            y_hy_c = None

        d_exp = jnp.repeat(ssd_d[i], SSD_P)[None, :]
        yf, yb = _ssd(xbc, dt, ssd_conv_w[i], ssd_conv_b[i][None, :],
                      _pad_lanes(ssd_dt_bias[i].reshape(1, -1)), _pad_lanes(ssd_a_log[i].reshape(1, -1)),
                      d_exp, ex, nlat // CHUNK)

        qz, kk, v0, v1 = _attn_prep(qkv, cos_t, sin_t, jnp.tile(attn_q_norm[i], AT_H)[None, :],
                                    jnp.tile(attn_k_norm[i], AT_KV)[None, :], gsum)
        y_at = _flash_attention(qz, kk, v0, v1, nlat, 256, t)
        if ctx_out:
            y_at_c = _flash_attention(qz[:, :, nlat:], kk[nlat:], v0[nlt:], v1[nlt:], CTX, 128, CTX)
        else:
            y_at_c = None

        x1, h2 = _out_proj(y_hy, y_hy_c, yf, yb, z, y_at, y_at_c, x_main, x_ctx, g1, sc2, sh2,
                           w_out[i].astype(BF16), ssd_norm_w[i][None, :], ln1_g[i][None, :],
                           ln1_b[i][None, :], with_ctx=ctx_out, alpha=alpha)
        ntiles = nlt + 1 if ctx_out else nlt
        x_main = _conv_ffn(h2, x1, g2, ffn_w_up[i].astype(BF16), ffn_conv_w[i], ffn_conv_b[i][None, :],
                           ffn_w_down[i].astype(BF16), ln2_g[i][None, :], ln2_b[i][None, :],
                           nlat_tiles=nlt, ntiles=ntiles, alpha=alpha)
        x_ctx = None
    return x_main[None]
```

```python
import functools
import math

import numpy as np
import jax
import jax.numpy as jnp
from jax import lax
from jax.experimental import pallas as pl
from jax.experimental.pallas import tpu as pltpu

F32 = jnp.float32
BF16 = jnp.bfloat16
HI = lax.Precision.HIGHEST

D_MODEL = 1024
DEPTH = 2
GRID_W = 64
CTX = 256
EPS = 1e-6

HY_W = 256
HY_BANDS = 16
HY_EMB = 1 + 2 * HY_BANDS
HY_HID = 64
HY_MIN_DECAY = math.log(1e-2) / 1.5
HY_MAX_DECAY = math.log(1e-2) / 0.3
HY_COLS = 3 * HY_W

SSD_H = 8
SSD_P = 64
SSD_W = SSD_H * SSD_P
SSD_G = 2
SSD_R = SSD_H // SSD_G
SSD_N = 128
SSD_XBC = SSD_W + 2 * SSD_G * SSD_N
CHUNK = 128

AT_H = 4
AT_KV = 2
AT_D = 64
AT_W = AT_H * AT_D
AT_KVW = AT_KV * AT_D
ROPE_BASE = 10000.0

MIX_W = HY_W + SSD_W + AT_W
D_FF = 2816
FF_CHUNK = 256

TM = CTX
LANES = 128
VT_ROWS = AT_D + 16
N2 = 256
V7X_VMEM_LIMIT = 56 * 1024 * 1024


def _cparams(sem, vmem=None):
    return pltpu.CompilerParams(dimension_semantics=sem, vmem_limit_bytes=vmem)


def _dot(a, b):
    return jnp.dot(a.astype(BF16), b.astype(BF16), preferred_element_type=F32)


def _dot_hi(a, b):
    return jnp.dot(a, b, preferred_element_type=F32, precision=HI)


def _split(x, n):
    pieces = []
    for _ in range(n):
        p = x.astype(BF16)
        pieces.append(p)
        x = x - p.astype(F32)
    return pieces


def _sigmoid(x):
    return 1.0 / (1.0 + jnp.exp(-x))


def _silu(x):
    return x * _sigmoid(x)


def _full(shape):
    nd = len(shape)
    return pl.BlockSpec(shape, lambda *_: (0,) * nd)


def _mod_body(c_ref, w_ref, b_ref, o_ref):
    s = _silu(c_ref[...])
    o_ref[0] = _dot(s, w_ref[0]) + b_ref[0]


def _modulation(cvec, w_mod, b_mod):
    depth, d, n = w_mod.shape
    nb = 1536
    return pl.pallas_call(
        _mod_body,
        grid=(depth, n // nb),
        in_specs=[pl.BlockSpec((8, d), lambda l, j: (0, 0)),
                  pl.BlockSpec((1, d, nb), lambda l, j: (l, 0, j)),
                  pl.BlockSpec((1, 1, nb), lambda l, j: (l, 0, j))],
        out_specs=pl.BlockSpec((1, 8, nb), lambda l, j: (l, 0, j)),
        out_shape=jax.ShapeDtypeStruct((depth, 8, n), F32),
        compiler_params=_cparams(("arbitrary", "arbitrary")),
        name="modulation",
    )(cvec, w_mod, b_mod.reshape(depth, 1, n))


W_MAIN = SSD_W + SSD_XBC + LANES + AT_W + 2 * AT_KVW


def _inproj_body(*refs, split_x, nlat_tiles):
    if split_x:
        x_ref, xc_ref, *refs = refs
        x = jnp.where(pl.program_id(0) == nlat_tiles, xc_ref[...], x_ref[...])
    else:
        x_ref, *refs = refs
        x = x_ref[...]
    sc_ref, sh_ref, wm_ref, wh_ref, hy_ref, z_ref, xbc_ref, dt_ref, qkv_ref = refs
    h = (x * (1.0 + sc_ref[0]) + sh_ref[0]).astype(BF16)
    main = jnp.dot(h, wm_ref[...], preferred_element_type=F32)
    z_ref[...] = main[:, 0:SSD_W]
    xbc_ref[...] = main[:, SSD_W:SSD_W + SSD_XBC]
    o = SSD_W + SSD_XBC
    dt_ref[...] = main[:, o:o + LANES]
    qkv_ref[...] = main[:, o + LANES:]
    hy_ref[...] = lax.dot_general(wh_ref[...], h, (((1,), (1,)), ((), ())),
                                  preferred_element_type=F32)


def _in_proj(x_main, x_ctx, t, sc, sh, w_main, w_hyT):
    nt = t // TM
    sel = lambda i: (jnp.where(i == nt - 1, 1, 0), 0, 0)
    row = lambda i: (i, 0)
    split_x = x_ctx is not None
    x_specs = [pl.BlockSpec((TM, D_MODEL), lambda i: (jnp.minimum(i, nt - 2), 0)),
               _full((TM, D_MODEL))] if split_x else [pl.BlockSpec((TM, D_MODEL), row)]
    x_args = [x_main, x_ctx] if split_x else [x_main]
    return pl.pallas_call(
        functools.partial(_inproj_body, split_x=split_x, nlat_tiles=nt - 1),
        grid=(nt,),
        in_specs=x_specs + [
                  pl.BlockSpec((1, 1, D_MODEL), sel),
                  pl.BlockSpec((1, 1, D_MODEL), sel),
                  _full((D_MODEL, W_MAIN)),
                  _full((HY_COLS, D_MODEL))],
        out_specs=[pl.BlockSpec((HY_COLS, TM), lambda i: (0, i)),
                   pl.BlockSpec((TM, SSD_W), row),
                   pl.BlockSpec((TM, SSD_XBC), row),
                   pl.BlockSpec((TM, LANES), row),
                   pl.BlockSpec((TM, AT_W + 2 * AT_KVW), row)],
        out_shape=[jax.ShapeDtypeStruct((HY_COLS, t), F32),
                   jax.ShapeDtypeStruct((t, SSD_W), F32),
                   jax.ShapeDtypeStruct((t, SSD_XBC), F32),
                   jax.ShapeDtypeStruct((t, LANES), F32),
                   jax.ShapeDtypeStruct((t, AT_W + 2 * AT_KVW), F32)],
        compiler_params=_cparams(("arbitrary",), V7X_VMEM_LIMIT),
        name="in_proj",
    )(*x_args, sc, sh, w_main, w_hyT)


def _filter_body(w1_ref, b1_ref, w2_ref, b2_ref, w3_ref, fr_ref, o_ref, *, n, nt):
    j = pl.program_id(0)
    k = (lax.broadcasted_iota(jnp.int32, (1, nt), 1) + j * nt).astype(F32)
    t = k * (1.0 / (n - 1))
    wk = k * (2.0 * math.pi / n)
    band = lax.broadcasted_iota(jnp.int32, (HY_BANDS, nt), 0).astype(F32)
    arg = (1e-4 + band * ((HY_BANDS - 1 - 1e-4) / (HY_BANDS - 1))) * wk
    r8 = lax.broadcasted_iota(jnp.int32, (8, nt), 0)
    feats = jnp.concatenate([jnp.cos(arg), -jnp.sin(arg), jnp.where(r8 == 0, t, 0.0),
                             jnp.zeros((LANES - 2 * HY_BANDS - 8, nt), F32)], axis=0)
    fr = fr_ref[...]
    h = jnp.sin(fr * (_dot_hi(w1_ref[...], feats) + b1_ref[...]))
    h = jnp.sin(fr * (_dot_hi(w2_ref[...], h) + b2_ref[...]))
    h = _dot(w3_ref[...], h)
    c = lax.broadcasted_iota(jnp.int32, (HY_W, nt), 0).astype(F32)
    delta = jnp.abs(HY_MIN_DECAY + c * ((HY_MAX_DECAY - HY_MIN_DECAY) / (HY_W - 1)))
    window = jnp.exp(-t * delta)
    for q in range(4):
        o_ref[q] = h[q * HY_W:(q + 1) * HY_W] * window


def _hyena_filters(n, w1t, b1, w2t, b2, w3t, freq):
    nt = min(n, 2048)
    return pl.pallas_call(
        functools.partial(_filter_body, n=n, nt=nt),
        grid=(n // nt,),
        in_specs=[_full((HY_HID, LANES)), _full((HY_HID, 1)), _full((HY_HID, HY_HID)),
                  _full((HY_HID, 1)), _full((4 * HY_W, HY_HID)), _full((HY_HID, 1))],
        out_specs=pl.BlockSpec((4, HY_W, nt), lambda j: (0, 0, j)),
        out_shape=jax.ShapeDtypeStruct((4, HY_W, n), F32),
        compiler_params=_cparams(("arbitrary",)),
        name="hyena_filters",
    )(w1t, b1, w2t, b2, w3t, freq)


def _fft_tables(n):
    big = 2 * n
    n1 = big // N2
    r1 = n1 // 2
    kp = -(-(r1 + 1) // 8) * 8
    k1 = np.arange(kp)[:, None].astype(np.float64)
    live = (k1 <= r1)
    a1 = 2 * np.pi * k1 * np.arange(r1)[None, :] / n1
    f1a = np.concatenate([np.cos(a1) * live, -np.sin(a1) * live], axis=0)
    at = 2 * np.pi * k1 * np.arange(N2)[None, :] / big
    twr, twi = np.cos(at) * live, -np.sin(at) * live
    a2 = 2 * np.pi * np.outer(np.arange(N2), np.arange(N2)) / N2
    c2, s2 = np.cos(a2), np.sin(a2)
    w2f = np.block([[c2, -s2], [s2, c2]])
    w2i = np.block([[c2, s2], [-s2, c2]])
    wgt = np.where((k1 == 0) | (k1 == r1), 1.0, 2.0) * live / big
    f1i = np.concatenate([(np.cos(a1) * wgt).T, (-np.sin(a1) * wgt).T], axis=1)
    j = jnp.asarray
    return dict(r1=r1, kp=kp, f1a=j(f1a, BF16), twr=j(twr, F32), twi=j(twi, F32),
                w2f=j(w2f, BF16), w2i=j(w2i, BF16), f1i=j(f1i, BF16))


def _fft_rows(x, f1a_ref, twr_ref, twi_ref, kp):
    a = jnp.dot(f1a_ref[...], x.astype(BF16), preferred_element_type=F32)
    ar, ai = a[:kp], a[kp:]
    twr, twi = twr_ref[...], twi_ref[...]
    return jnp.concatenate([ar * twr - ai * twi, ar * twi + ai * twr], axis=1)


def _kspec_body(hf_ref, hb_ref, f1a_ref, twr_ref, twi_ref, w2f_ref, o_ref, sa_ref, *, cb, kp, r1):
    row = lax.broadcasted_iota(jnp.int32, (r1, N2), 0)
    lane = lax.broadcasted_iota(jnp.int32, (r1, N2), 1)
    first = (row == 0) & (lane == 0)

    for c in range(cb):
        af = _fft_rows(hf_ref[0, c], f1a_ref, twr_ref, twi_ref, kp)
        ab = _fft_rows(jnp.where(first, 0.0, hb_ref[0, c]), f1a_ref, twr_ref, twi_ref, kp)
        sa_ref[c * kp:(c + 1) * kp, :] = af + ab
        sa_ref[(cb + c) * kp:(cb + c + 1) * kp, :] = af - ab
    kr = jnp.dot(sa_ref[0:cb * kp, :].astype(BF16), w2f_ref[:, 0:N2], preferred_element_type=F32)
    ki = jnp.dot(sa_ref[cb * kp:, :].astype(BF16), w2f_ref[:, N2:], preferred_element_type=F32)
    o_ref[0] = jnp.concatenate([kr, ki], axis=1).reshape(cb, kp, 2 * N2)


def _filter_spectra(filt, tb, cb):
    r1, kp = tb["r1"], tb["kp"]
    return pl.pallas_call(
        functools.partial(_kspec_body, cb=cb, kp=kp, r1=r1),
        grid=(2, HY_W // cb),
        in_specs=[pl.BlockSpec((1, cb, r1, N2), lambda o, j: (2 * o, j, 0, 0)),
                  pl.BlockSpec((1, cb, r1, N2), lambda o, j: (2 * o + 1, j, 0, 0)),
                  _full((2 * kp, r1)), _full((kp, N2)), _full((kp, N2)), _full((2 * N2, 2 * N2))],
        out_specs=pl.BlockSpec((1, cb, kp, 2 * N2), lambda o, j: (o, j, 0, 0)),
        out_shape=jax.ShapeDtypeStruct((2, HY_W, kp, 2 * N2), F32),
        scratch_shapes=[pltpu.VMEM((2 * cb * kp, 2 * N2), F32)],
        compiler_params=_cparams(("arbitrary", "arbitrary"), V7X_VMEM_LIMIT),
        name="hyena_filter_spectra",
    )(filt, filt, tb["f1a"], tb["twr"], tb["twi"], tb["w2f"])


def _short_conv(x, w0, w1, w2, b):
    r1 = x.shape[0]
    row = lax.broadcasted_iota(jnp.int32, x.shape, 0)
    lane = lax.broadcasted_iota(jnp.int32, x.shape, 1)
    a = pltpu.roll(x, 1, 1)
    prev = jnp.where(lane == 0, jnp.where(row == 0, 0.0, pltpu.roll(a, 1, 0)), a)
    a = pltpu.roll(x, N2 - 1, 1)
    nxt = jnp.where(lane == N2 - 1, jnp.where(row == r1 - 1, 0.0, pltpu.roll(a, r1 - 1, 0)), a)
    return w0 * prev + w1 * x + w2 * nxt + b


def _hyconv_body(taps_ref, skip_ref, u_ref, g_ref, kf_ref, f1a_ref, twr_ref, twi_ref, w2f_ref,
                 w2i_ref, f1i_ref, o_ref, sa_ref, sq_ref, sx_ref, *, cb, kp, conv_u, u_off,
                 g_off, order):
    j = pl.program_id(0)

    def taps(ch):
        return (taps_ref[ch * 4], taps_ref[ch * 4 + 1], taps_ref[ch * 4 + 2], taps_ref[ch * 4 + 3])

    for c in range(cb):
        x = u_ref[c]
        if conv_u:
            x = _short_conv(x, *taps(u_off + j * cb + c))
        sx_ref[c] = x
        sa_ref[c * kp:(c + 1) * kp, :] = _fft_rows(x, f1a_ref, twr_ref, twi_ref, kp)
    x = jnp.dot(sa_ref[...].astype(BF16), w2f_ref[...], preferred_element_type=F32)
    k = kf_ref[0].reshape(cb * kp, 2 * N2)
    xr, xi, kr, ki = x[:, :N2], x[:, N2:], k[:, :N2], k[:, N2:]
    p = jnp.concatenate([xr * kr - xi * ki, xr * ki + xi * kr], axis=1).astype(BF16)
    sq_ref[...] = jnp.dot(p, w2i_ref[...], preferred_element_type=F32)

    for c in range(cb):
        q = sq_ref[c * kp:(c + 1) * kp, :]
        qr, qi = q[:, :N2], q[:, N2:]
        twr, twi = twr_ref[...], twi_ref[...]
        y2 = jnp.concatenate([qr * twr + qi * twi, qi * twr - qr * twi], axis=0).astype(BF16)
        y = jnp.dot(f1i_ref[...], y2, preferred_element_type=F32)
        y = y + skip_ref[order * HY_W + j * cb + c] * sx_ref[c]
        o_ref[c] = _short_conv(g_ref[c], *taps(g_off + j * cb + c)) * y


def _hyena_conv(taps, skip, u3, u_blk0, g3, g_blk0, kf, tb, cb, *, conv_u, u_off, g_off, order):
    r1, kp = tb["r1"], tb["kp"]
    smem = pl.BlockSpec(memory_space=pltpu.SMEM)
    return pl.pallas_call(
        functools.partial(_hyconv_body, cb=cb, kp=kp, conv_u=conv_u, u_off=u_off, g_off=g_off,
                          order=order),
        grid=(HY_W // cb,),
        in_specs=[smem, smem,
                  pl.BlockSpec((cb, r1, N2), lambda j: (j + u_blk0, 0, 0)),
                  pl.BlockSpec((cb, r1, N2), lambda j: (j + g_blk0, 0, 0)),
                  pl.BlockSpec((1, cb, kp, 2 * N2), lambda j: (order, j, 0, 0)),
                  _full((2 * kp, r1)), _full((kp, N2)), _full((kp, N2)),
                  _full((2 * N2, 2 * N2)), _full((2 * N2, 2 * N2)), _full((r1, 2 * kp))],
        out_specs=pl.BlockSpec((cb, r1, N2), lambda j: (j, 0, 0)),
        out_shape=jax.ShapeDtypeStruct((HY_W, r1, N2), F32),
        scratch_shapes=[pltpu.VMEM((cb * kp, 2 * N2), F32),
                        pltpu.VMEM((cb * kp, 2 * N2), F32),
                        pltpu.VMEM((cb, r1, N2), F32)],
        compiler_params=_cparams(("arbitrary",), V7X_VMEM_LIMIT),
        name="hyena_conv%d" % order,
    )(taps, skip, u3, g3, kf, tb["f1a"], tb["twr"], tb["twi"], tb["w2f"], tb["w2i"], tb["f1i"])


def _ctx_tables(n):
    big = 2 * n
    a = 2 * np.pi * np.outer(np.arange(n), np.arange(big)) / big
    fc = np.concatenate([np.cos(a), -np.sin(a)], axis=1)
    gi = np.concatenate([np.cos(a).T, -np.sin(a).T], axis=0) / big
    return jnp.asarray(fc, BF16), jnp.asarray(gi, BF16)


def _hyctx_body(hy_ref, filt_ref, taps_ref, skip_ref, fc_ref, gi_ref, o_ref, *, n):
    big = 2 * n
    lane = lax.broadcasted_iota(jnp.int32, (HY_W, n), 1)

    def sconv(x, t):
        prev = jnp.where(lane == 0, 0.0, pltpu.roll(x, 1, 1))
        nxt = jnp.where(lane == n - 1, 0.0, pltpu.roll(x, n - 1, 1))
        return t[:, 0:1] * prev + t[:, 1:2] * x + t[:, 2:3] * nxt + t[:, 3:4]

    def spec(o):
        xf = _dot(filt_ref[2 * o], fc_ref[...])
        xb = _dot(jnp.where(lane == 0, 0.0, filt_ref[2 * o + 1]), fc_ref[...])
        return xf[:, :big] + xb[:, :big], xf[:, big:] - xb[:, big:]

    def lconv(u, o):
        kr, ki = spec(o)
        x = _dot(u, fc_ref[...])
        xr, xi = x[:, :big], x[:, big:]
        p = jnp.concatenate([xr * kr - xi * ki, xr * ki + xi * kr], axis=1)
        return _dot(p, gi_ref[...]) + skip_ref[o * HY_W:(o + 1) * HY_W, :] * u

    v = sconv(hy_ref[0:HY_W, :], taps_ref[0:HY_W, :])
    x1 = sconv(hy_ref[HY_W:2 * HY_W, :], taps_ref[HY_W:2 * HY_W, :])
    x2 = sconv(hy_ref[2 * HY_W:, :], taps_ref[2 * HY_W:, :])
    z = x1 * lconv(v, 0)
    o_ref[...] = x2 * lconv(z, 1)


def _hyena_ctx(hyT, blk, filt_ctx, taps2d, skip2d):
    n = filt_ctx.shape[-1]
    fc, gi = _ctx_tables(n)
    return pl.pallas_call(
        functools.partial(_hyctx_body, n=n),
        grid=(1,),
        in_specs=[pl.BlockSpec((HY_COLS, n), lambda i: (0, blk)),
                  _full((4, HY_W, n)), _full((HY_COLS, 4)), _full((2 * HY_W, 1)),
                  _full((n, 4 * n)), _full((4 * n, n))],
        out_specs=_full((HY_W, n)),
        out_shape=jax.ShapeDtypeStruct((HY_W, n), F32),
        compiler_params=_cparams(("arbitrary",), V7X_VMEM_LIMIT),
        name="hyena_ctx",
    )(hyT, filt_ctx, taps2d, skip2d, fc, gi)


def _ssd_body(xf_ref, xfp_ref, xfn_ref, dtf_ref, xb_ref, xbp_ref, xbn_ref, dtb_ref,
              cw_ref, cb_ref, dtbias_ref, alog_ref, dexp_ref, ex_ref,
              yf_ref, yb_ref, h_ref, sx_ref, *, nlat):
    j = pl.program_id(0)
    nch = nlat + CTX // CHUNK

    @pl.when(j == 0)
    def _():
        h_ref[...] = jnp.zeros_like(h_ref)

    ri = lax.broadcasted_iota(jnp.int32, (CHUNK, CHUNK), 0)
    ci = lax.broadcasted_iota(jnp.int32, (CHUNK, CHUNK), 1)
    a_all = -jnp.exp(alog_ref[...])

    for d in range(2):
        x_ref, xp_ref, xn_ref, dt_ref, y_ref = (
            (xf_ref, xfp_ref, xfn_ref, dtf_ref, yf_ref) if d == 0
            else (xb_ref, xbp_ref, xbn_ref, dtb_ref, yb_ref))
        cid = jnp.where(j < 2, nlat + j, j - 2) if d == 0 else nch - 1 - j
        pv = ((cid != 0) & (cid != nlat)).astype(F32)
        nv = ((cid != nlat - 1) & (cid != nch - 1)).astype(F32)
        sx_ref[0:8, :] = xp_ref[...] * pv
        sx_ref[8:8 + CHUNK, :] = x_ref[...]
        sx_ref[8 + CHUNK:, :] = xn_ref[...] * nv
        pre = (cw_ref[0:1, :] * sx_ref[pl.ds(7, CHUNK), :] + cw_ref[1:2, :] * x_ref[...]
               + cw_ref[2:3, :] * sx_ref[pl.ds(9, CHUNK), :] + cb_ref[...])
        xc = _silu(pre)
        xs = xc[:, 0:SSD_W]
        z = dt_ref[...] + dtbias_ref[...]
        dtv = jnp.maximum(z, 0.0) + jnp.log(1.0 + jnp.exp(-jnp.abs(z)))
        da = dtv * a_all
        tri = (ri >= ci) if d == 0 else (ri <= ci)
        r3 = jnp.dot(tri.astype(BF16), jnp.concatenate(_split(da, 3), axis=1),
                     preferred_element_type=F32)
        acum = r3[:, 0:LANES] + r3[:, LANES:2 * LANES] + r3[:, 2 * LANES:]
        last = CHUNK - 1 if d == 0 else 0
        acum_t = acum.T
        pieces = jnp.concatenate(_split(acum, 2) + _split(dtv, 2), axis=1)
        ys = []
        for g in range(SSD_G):
            bm = xc[:, SSD_W + g * SSD_N:SSD_W + (g + 1) * SSD_N]
            cm = xc[:, SSD_W + (SSD_G + g) * SSD_N:SSD_W + (SSD_G + g + 1) * SSD_N]
            bt = bm.T
            scores = _dot(cm, bt)
            both = jnp.dot(pieces, ex_ref[d * SSD_G + g], preferred_element_type=F32)
            acum_e, dt_e = both[:, 0:SSD_R * SSD_P], both[:, SSD_R * SSD_P:]
            tot_e = acum_e[last:last + 1, :]
            xg = xs[:, g * SSD_R * SSD_P:(g + 1) * SSD_R * SSD_P]
            hg = h_ref[d, g]
            yoff = _dot(cm, hg) * jnp.exp(acum_e)
            ydiag = []
            for r in range(SSD_R):
                hl = d * SSD_H + g * SSD_R + r
                seg = acum[:, hl:hl + 1] - acum_t[hl:hl + 1, :]
                m = scores * jnp.exp(jnp.where(tri, seg, -jnp.inf))
                xdt = xg[:, r * SSD_P:(r + 1) * SSD_P] * dt_e[:, r * SSD_P:(r + 1) * SSD_P]
                ydiag.append(_dot(m, xdt))
            ys.append(jnp.concatenate(ydiag, axis=1) + yoff)
            xw = xg * (jnp.exp(tot_e - acum_e) * dt_e)
            h_ref[d, g] = hg * jnp.exp(tot_e) + _dot(bt, xw)
        y = jnp.concatenate(ys, axis=1)
        if d == 0:
            y = y + dexp_ref[...] * xs
        y_ref[...] = y


def _ssd(xbc, dt, conv_w, conv_b, dt_bias, a_log, d_exp, ex, nlat):
    t = xbc.shape[0]
    nch = t // CHUNK
    hb = CHUNK // 8
    last8 = t // 8 - 1
    fc = lambda j: jnp.where(j < 2, nlat + j, j - 2)
    bc = lambda j: nch - 1 - j
    prev = lambda f: (lambda j: (jnp.maximum(f(j) * hb - 1, 0), 0))
    nxt = lambda f: (lambda j: (jnp.minimum((f(j) + 1) * hb, last8), 0))
    main = lambda f: (lambda j: (f(j), 0))
    return pl.pallas_call(
        functools.partial(_ssd_body, nlat=nlat),
        grid=(nch,),
        in_specs=[pl.BlockSpec((CHUNK, SSD_XBC), main(fc)), pl.BlockSpec((8, SSD_XBC), prev(fc)),
                  pl.BlockSpec((8, SSD_XBC), nxt(fc)), pl.BlockSpec((CHUNK, LANES), main(fc)),
                  pl.BlockSpec((CHUNK, SSD_XBC), main(bc)), pl.BlockSpec((8, SSD_XBC), prev(bc)),
                  pl.BlockSpec((8, SSD_XBC), nxt(bc)), pl.BlockSpec((CHUNK, LANES), main(bc)),
                  _full((3, SSD_XBC)), _full((1, SSD_XBC)), _full((1, LANES)), _full((1, LANES)),
                  _full((1, SSD_W)), _full((2 * SSD_G, 4 * LANES, 2 * SSD_R * SSD_P))],
        out_specs=[pl.BlockSpec((CHUNK, SSD_W), main(fc)), pl.BlockSpec((CHUNK, SSD_W), main(bc))],
        out_shape=[jax.ShapeDtypeStruct((t, SSD_W), F32), jax.ShapeDtypeStruct((t, SSD_W), F32)],
        scratch_shapes=[pltpu.VMEM((2, SSD_G, SSD_N, SSD_R * SSD_P), F32),
                        pltpu.VMEM((CHUNK + 16, SSD_XBC), F32)],
        compiler_params=_cparams(("arbitrary",), V7X_VMEM_LIMIT),
        name="ssd_scan",
    )(xbc, xbc, xbc, dt, xbc, xbc, xbc, dt, conv_w, conv_b, dt_bias, a_log, d_exp, ex)


def _attn_prep_body(qkv_ref, cos_ref, sin_ref, qn_ref, kn_ref, gsum_ref, q_ref, k_ref, v0_ref, v1_ref):
    x = qkv_ref[...]
    qk = x[:, 0:AT_W + AT_KVW]
    ms = jnp.dot(jnp.concatenate(_split(qk * qk, 2), axis=1), gsum_ref[...],
                 preferred_element_type=F32)
    wn = jnp.concatenate([qn_ref[...], kn_ref[...]], axis=1)
    qk = qk * lax.rsqrt(ms + EPS) * wn
    w = qk.shape[1]
    lane = lax.broadcasted_iota(jnp.int32, qk.shape, 1)
    partner = jnp.where(lane % AT_D < AT_D // 2, pltpu.roll(qk, w - AT_D // 2, 1),
                        pltpu.roll(qk, AT_D // 2, 1))
    cos = jnp.concatenate([cos_ref[...]] * 3, axis=1)
    sin = jnp.concatenate([sin_ref[...]] * 3, axis=1)
    qk = qk * cos + partner * sin
    q = qk[:, 0:AT_W] * (AT_D ** -0.5 * math.log2(math.e))
    zeros = jnp.zeros((q.shape[0], AT_D), F32)
    for h in range(AT_H):
        qh = q[:, h * AT_D:(h + 1) * AT_D]
        qh = (jnp.concatenate([qh, zeros], axis=1) if h // 2 == 0
              else jnp.concatenate([zeros, qh], axis=1))
        q_ref[h] = qh.T.astype(BF16)
    k_ref[...] = qk[:, AT_W:].astype(BF16)
    v = x[:, AT_W + AT_KVW:]
    vl = lax.broadcasted_iota(jnp.int32, v.shape, 1)
    for g, v_ref in enumerate((v0_ref, v1_ref)):
        vg = v if g == 0 else pltpu.roll(v, AT_D, 1)
        vg = jnp.where(vl < AT_D, vg, jnp.where(vl == AT_D, 1.0, 0.0))
        v_ref[0] = vg.T[0:VT_ROWS].astype(BF16)


def _attn_prep(qkv, cos_t, sin_t, qn, kn, gsum):
    t = qkv.shape[0]
    row = lambda i: (i, 0)
    return pl.pallas_call(
        _attn_prep_body,
        grid=(t // TM,),
        in_specs=[pl.BlockSpec((TM, AT_W + 2 * AT_KVW), row), pl.BlockSpec((TM, LANES), row),
                  pl.BlockSpec((TM, LANES), row), _full((1, AT_W)), _full((1, AT_KVW)),
                  _full((2 * (AT_W + AT_KVW), AT_W + AT_KVW))],
        out_specs=[pl.BlockSpec((AT_H, LANES, TM), lambda i: (0, 0, i)),
                   pl.BlockSpec((TM, LANES), row),
                   pl.BlockSpec((1, VT_ROWS, TM), lambda i: (i, 0, 0)),
                   pl.BlockSpec((1, VT_ROWS, TM), lambda i: (i, 0, 0))],
        out_shape=[jax.ShapeDtypeStruct((AT_H, LANES, t), BF16),
                   jax.ShapeDtypeStruct((t, LANES), BF16),
                   jax.ShapeDtypeStruct((t // TM, VT_ROWS, TM), BF16),
                   jax.ShapeDtypeStruct((t // TM, VT_ROWS, TM), BF16)],
        compiler_params=_cparams(("arbitrary",)),
        name="attn_prep",
    )(qkv, cos_t, sin_t, qn, kn, gsum)


def _flash_body(q_ref, k_ref, v0_ref, v1_ref, o_ref, s_ref, *, tq, kc, nkc):
    v_refs = (v0_ref, v1_ref)
    nb = kc // TM
    qs = [jnp.concatenate([q_ref[2 * g], q_ref[2 * g + 1]], axis=1) for g in range(AT_KV)]

    def scores(c, slot):
        k = k_ref[pl.ds(pl.multiple_of(c * kc, kc), kc), :]
        for g in range(AT_KV):
            s_ref[slot, g] = jnp.dot(k, qs[g], preferred_element_type=F32)

    def update(c, slot, carry):
        out = []
        for g, (m, acc) in enumerate(carry):
            s = s_ref[slot, g]
            m_new = jnp.maximum(m, jnp.max(s, axis=0, keepdims=True))
            p = jnp.exp2(s - m_new).astype(BF16)
            pv = jnp.dot(v_refs[g][c * nb], p[0:TM], preferred_element_type=F32)
            for b in range(1, nb):
                pv = pv + jnp.dot(v_refs[g][c * nb + b], p[b * TM:(b + 1) * TM],
                                  preferred_element_type=F32)
            out.append((m_new, jnp.exp2(m - m_new) * acc + pv))
        return tuple(out)

    def pair(i, carry):
        scores(2 * i + 1, 1)
        carry = update(2 * i, 0, carry)
        scores(2 * i + 2, 0)
        return update(2 * i + 1, 1, carry)

    scores(0, 0)
    carry = tuple((jnp.full((1, 2 * tq), -jnp.inf, F32), jnp.zeros((VT_ROWS, 2 * tq), F32))
                  for _ in range(AT_KV))
    carry = lax.fori_loop(0, (nkc - 1) // 2, pair, carry)
    if (nkc - 1) % 2:
        scores(nkc - 1, 1)
        carry = update(nkc - 2, 0, carry)
        carry = update(nkc - 1, 1, carry)
    else:
        carry = update(nkc - 1, 0, carry)
    lane = lax.broadcasted_iota(jnp.int32, (tq, LANES), 1)
    outs = []
    for _, acc in carry:
        o = acc[0:AT_D] / acc[AT_D:AT_D + 1, :]
        o = jnp.concatenate([o, jnp.zeros((LANES - AT_D, 2 * tq), F32)], axis=0).T
        outs.append(jnp.where(lane < AT_D, o[:tq], pltpu.roll(o[tq:], AT_D, 1)))
    o_ref[...] = jnp.concatenate(outs, axis=1)


def _flash_attention(qt, k, v0t, v1t, nq_rows, tq, kv_rows, q_blk0=0):
    nd = kv_rows // TM
    kc = TM * max(d for d in range(1, 6) if nd % d == 0)
    kspec = pl.BlockSpec((kv_rows, LANES), lambda i: (0, 0))
    vspec = pl.BlockSpec((nd, VT_ROWS, TM), lambda i: (0, 0, 0))
    return pl.pallas_call(
        functools.partial(_flash_body, tq=tq, kc=kc, nkc=kv_rows // kc),
        grid=(nq_rows // tq,),
        in_specs=[pl.BlockSpec((AT_H, LANES, tq), lambda i: (0, 0, i + q_blk0)),
                  kspec, vspec, vspec],
        out_specs=pl.BlockSpec((tq, AT_W), lambda i: (i, 0)),
        out_shape=jax.ShapeDtypeStruct((nq_rows, AT_W), F32),
        scratch_shapes=[pltpu.VMEM((2, AT_KV, kc, 2 * tq), F32)],
        compiler_params=_cparams(("arbitrary",), V7X_VMEM_LIMIT),
        name="flash_attention",
    )(qt, k, v0t, v1t)


def _layernorm(x, g, b):
    mu = jnp.mean(x, axis=-1, keepdims=True)
    xc = x - mu
    var = jnp.mean(xc * xc, axis=-1, keepdims=True)
    return xc * lax.rsqrt(var + EPS) * g + b


def _outproj_body(*refs, with_ctx, split_x, nlat_tiles, alpha):
    refs = list(refs)
    take = lambda n: [refs.pop(0) for _ in range(n)]
    hyl_ref, = take(1)
    hyc_ref, = take(1) if with_ctx else (None,)
    yf_ref, yb_ref, z_ref, atl_ref = take(4)
    atc_ref, = take(1) if with_ctx else (None,)
    x_ref, = take(1)
    xc_ref, = take(1) if split_x else (None,)
    g1_ref, sc2_ref, sh2_ref, w_ref, nw_ref, lg_ref, lb_ref, x1_ref, h2_ref = refs
    hy, at, x = hyl_ref[...], atl_ref[...], x_ref[...]
    if with_ctx:
        is_ctx = pl.program_id(0) == nlat_tiles
        hy = jnp.where(is_ctx, hyc_ref[...], hy)
        at = jnp.where(is_ctx, atc_ref[...], at)
        if split_x:
            x = jnp.where(is_ctx, xc_ref[...], x)
    m = _dot(hy.T, w_ref[0:HY_W, :])
    gated = (yf_ref[...] + yb_ref[...]) * _silu(z_ref[...])
    gw = SSD_W // SSD_G
    parts = []
    for g in range(SSD_G):
        gg = gated[:, g * gw:(g + 1) * gw]
        parts.append(gg * lax.rsqrt(jnp.mean(gg * gg, axis=-1, keepdims=True) + EPS))
    ssd = jnp.concatenate(parts, axis=1) * nw_ref[...]
    m = m + _dot(ssd, w_ref[HY_W:HY_W + SSD_W, :]) + _dot(at, w_ref[HY_W + SSD_W:, :])
    x1 = _layernorm(alpha * x + g1_ref[0] * m, lg_ref[...], lb_ref[...])
    x1_ref[...] = x1
    h2_ref[...] = (x1 * (1.0 + sc2_ref[0]) + sh2_ref[0]).astype(BF16)


def _out_proj(hy_lat, hy_ctx, yf, yb, z, at_lat, at_ctx, x_main, x_ctx, g1, sc2, sh2, w_out, nw, lg, lb,
              *, with_ctx, alpha):
    t = yf.shape[0]
    split_x = with_ctx and x_ctx is not None
    nlat_tiles = hy_lat.shape[1] // TM
    nt = t // TM if with_ctx else nlat_tiles
    last = t // TM - 1
    sel = lambda i: (jnp.where(i == last, 1, 0), 0, 0)
    row = lambda i: (i, 0)
    lat_row = lambda i: (jnp.minimum(i, nlat_tiles - 1), 0)
    lat_col = lambda i: (0, jnp.minimum(i, nlat_tiles - 1))
    vec = pl.BlockSpec((1, 1, D_MODEL), sel)
    specs = [pl.BlockSpec((HY_W, TM), lat_col)]
    args = [hy_lat]
    if with_ctx:
        specs.append(_full((HY_W, TM)))
        args.append(hy_ctx)
    specs += [pl.BlockSpec((TM, SSD_W), row)] * 3 + [pl.BlockSpec((TM, AT_W), lat_row)]
    args += [yf, yb, z, at_lat]
    if with_ctx:
        specs.append(_full((TM, AT_W)))
        args.append(at_ctx)
    specs.append(pl.BlockSpec((TM, D_MODEL), lat_row if x_ctx is not None else row))
    args.append(x_main)
    if split_x:
        specs.append(_full((TM, D_MODEL)))
        args.append(x_ctx)
    specs += [vec, vec, vec, _full((MIX_W, D_MODEL)),
              _full((1, SSD_W)), _full((1, D_MODEL)), _full((1, D_MODEL))]
    args += [g1, sc2, sh2, w_out, nw, lg, lb]
    return pl.pallas_call(
        functools.partial(_outproj_body, with_ctx=with_ctx, split_x=split_x, nlat_tiles=nlat_tiles,
                          alpha=alpha),
        grid=(nt,),
        in_specs=specs,
        out_specs=[pl.BlockSpec((TM, D_MODEL), row), pl.BlockSpec((TM, D_MODEL), row)],
        out_shape=[jax.ShapeDtypeStruct((nt * TM, D_MODEL), F32),
                   jax.ShapeDtypeStruct((nt * TM, D_MODEL), BF16)],
        compiler_params=_cparams(("arbitrary",), V7X_VMEM_LIMIT),
        name="out_proj",
    )(*args)


HALO = 16


def _ffn_body(h_ref, hp_ref, hn_ref, x1_ref, g2_ref, wup_ref, cw_ref, cb_ref, wdn_ref, lg_ref, lb_ref,
              o_ref, *, nlat_tiles, ntiles, alpha):
    i = pl.program_id(0)
    pv = (i != 0) & (i != nlat_tiles)
    nv = (i != nlat_tiles - 1) & (i != ntiles - 1)
    zero = jnp.zeros((HALO, D_MODEL), BF16)
    hp = jnp.where(pv, hp_ref[...], zero)
    hn = jnp.where(nv, hn_ref[...], zero)
    hext = jnp.concatenate([hp, h_ref[...], hn], axis=0)
    acc = jnp.zeros((TM, D_MODEL), F32)
    nchunks = D_FF // FF_CHUNK

    def up(c):
        return [jnp.dot(hext, wup_ref[:, o:o + FF_CHUNK], preferred_element_type=F32)
                for o in (c * FF_CHUNK, D_FF + c * FF_CHUNK)]

    def conv(u, o):
        cw = cw_ref[:, o:o + FF_CHUNK]
        prev = pltpu.roll(u, 1, 0)[HALO:HALO + TM]
        nxt = pltpu.roll(u, TM + 2 * HALO - 1, 0)[HALO:HALO + TM]
        return (cw[0:1] * prev + cw[1:2] * u[HALO:HALO + TM] + cw[2:3] * nxt
                + cb_ref[:, o:o + FF_CHUNK])

    ahead = 2
    queue = [up(c) for c in range(ahead)]
    for c in range(nchunks):
        if c + ahead < nchunks:
            queue.append(up(c + ahead))
        us = queue.pop(0)
        c0 = c * FF_CHUNK
        act = (_silu(conv(us[0], c0)) * conv(us[1], D_FF + c0)).astype(BF16)
        acc = acc + jnp.dot(act, wdn_ref[c0:c0 + FF_CHUNK, :], preferred_element_type=F32)
    o_ref[...] = _layernorm(alpha * x1_ref[...] + g2_ref[0] * acc, lg_ref[...], lb_ref[...])


def _conv_ffn(h2, x1, g2, w_up, conv_w, conv_b, w_down, lg, lb, *, nlat_tiles, ntiles, alpha):
    hb = TM // HALO
    lastb = ntiles * hb - 1
    row = lambda i: (i, 0)
    sel = lambda i: (jnp.where(i == nlat_tiles, 1, 0), 0, 0)
    return pl.pallas_call(
        functools.partial(_ffn_body, nlat_tiles=nlat_tiles, ntiles=ntiles, alpha=alpha),
        grid=(ntiles,),
        in_specs=[pl.BlockSpec((TM, D_MODEL), row),
                  pl.BlockSpec((HALO, D_MODEL), lambda i: (jnp.maximum(i * hb - 1, 0), 0)),
                  pl.BlockSpec((HALO, D_MODEL), lambda i: (jnp.minimum((i + 1) * hb, lastb), 0)),
                  pl.BlockSpec((TM, D_MODEL), row),
                  pl.BlockSpec((1, 1, D_MODEL), sel),
                  _full((D_MODEL, 2 * D_FF)), _full((3, 2 * D_FF)), _full((1, 2 * D_FF)),
                  _full((D_FF, D_MODEL)), _full((1, D_MODEL)), _full((1, D_MODEL))],
        out_specs=pl.BlockSpec((TM, D_MODEL), row),
        out_shape=jax.ShapeDtypeStruct((ntiles * TM, D_MODEL), F32),
        compiler_params=_cparams(("arbitrary",), V7X_VMEM_LIMIT),
        name="conv_ffn",
    )(h2, h2, h2, x1, g2, w_up, conv_w, conv_b, w_down, lg, lb)


def _rope_tables(nlat):
    rows = nlat // GRID_W
    row = jnp.repeat(jnp.arange(rows, dtype=F32), GRID_W)
    col = jnp.tile(jnp.arange(GRID_W, dtype=F32), rows)
    nf = AT_D // 4
    inv = ROPE_BASE ** (-jnp.arange(nf, dtype=F32) / nf)
    ang = jnp.concatenate([row[:, None] * inv, col[:, None] * inv], axis=-1)
    cos, sin = jnp.cos(ang), jnp.sin(ang)
    cos_h = jnp.concatenate([cos, cos], axis=1)
    sin_h = jnp.concatenate([-sin, sin], axis=1)
    cos_t = jnp.concatenate([jnp.tile(cos_h, (1, LANES // AT_D)), jnp.ones((CTX, LANES), F32)], axis=0)
    sin_t = jnp.concatenate([jnp.tile(sin_h, (1, LANES // AT_D)), jnp.zeros((CTX, LANES), F32)], axis=0)
    return cos_t, sin_t


def _expansion():
    w = SSD_R * SSD_P
    ex = np.zeros((2 * SSD_G, 4 * LANES, 2 * w), np.float32)
    for d in range(2):
        for g in range(SSD_G):
            e = np.zeros((LANES, w), np.float32)
            for r in range(SSD_R):
                e[d * SSD_H + g * SSD_R + r, r * SSD_P:(r + 1) * SSD_P] = 1.0
            for piece in range(4):
                ex[d * SSD_G + g, piece * LANES:(piece + 1) * LANES,
                   (piece // 2) * w:(piece // 2 + 1) * w] = e
    return jnp.asarray(ex, BF16)


def _group_mean_matrix():
    w = AT_W + AT_KVW
    idx = np.arange(w) // AT_D
    g = (idx[:, None] == idx[None, :]).astype(np.float32) / AT_D
    return jnp.asarray(np.concatenate([g, g], axis=0), BF16)


def _pad_lanes(v, width=LANES):
    return jnp.pad(v, ((0, 0), (0, width - v.shape[1])))


def kernel(x, c, ctx, c_ctx, w_mod, b_mod, w_in, hy_conv_w, hy_conv_b, hy_ffn_w1, hy_ffn_b1, hy_ffn_w2, hy_ffn_b2, hy_ffn_w3, hy_freq, hy_bias, ssd_conv_w, ssd_conv_b, ssd_dt_bias, ssd_a_log, ssd_d, ssd_norm_w, attn_q_norm, attn_k_norm, w_out, ln1_g, ln1_b, ffn_w_up, ffn_conv_w, ffn_conv_b, ffn_w_down, ln2_g, ln2_b):
    nlat = x.shape[1]
    assert x.shape[0] == 1 and ctx.shape[1] == CTX and nlat % (N2 * 8) == 0
    t = nlat + CTX
    nlt = nlat // TM
    alpha = (2.0 * DEPTH) ** 0.25
    cb = 16

    x_main, x_ctx = x[0], ctx[0]
    cvec = jnp.zeros((8, D_MODEL), F32).at[0].set(c[0]).at[1].set(c_ctx)
    mod = _modulation(cvec, w_mod, b_mod)
    cos_t, sin_t = _rope_tables(nlat)
    ex = _expansion()
    gsum = _group_mean_matrix()
    tb = _fft_tables(nlat)

    for i in range(DEPTH):
        ctx_out = i < DEPTH - 1
        mv = mod[i, 0:2].reshape(2, 6, 1, D_MODEL)
        sh1, sc1, g1, sh2, sc2, g2 = (mv[:, q] for q in range(6))

        wi = w_in[i]
        o = HY_COLS
        w_z = wi[:, o:o + SSD_W]
        w_xbc = wi[:, o + SSD_W:o + SSD_W + SSD_XBC]
        w_dt = _pad_lanes(wi[:, o + SSD_W + SSD_XBC:o + SSD_W + SSD_XBC + 2 * SSD_H])
        w_at = wi[:, o + SSD_W + SSD_XBC + 2 * SSD_H:]
        w_main = jnp.concatenate([w_z, w_xbc, w_dt, w_at], axis=1).astype(BF16)
        w_hyT = wi[:, 0:HY_COLS].T.astype(BF16)

        hyT, z, xbc, dt, qkv = _in_proj(x_main, x_ctx, t, sc1, sh1, w_main, w_hyT)

        w1 = hy_ffn_w1[i]
        w1t = _pad_lanes(jnp.concatenate([w1[1:], w1[0:1]], axis=0).T)
        fargs = (w1t, hy_ffn_b1[i][:, None], hy_ffn_w2[i].T, hy_ffn_b2[i][:, None],
                 hy_ffn_w3[i].T, hy_freq[i][:, None])
        taps2d = jnp.concatenate([hy_conv_w[i].T, hy_conv_b[i][:, None]], axis=1)
        taps = taps2d.reshape(-1)
        skip = hy_bias[i].reshape(-1)
        filt = _hyena_filters(nlat, *fargs).reshape(4, HY_W, tb["r1"], N2)
        kf = _filter_spectra(filt, tb, cb)
        hy3 = hyT.reshape(HY_COLS, t // N2, N2)
        zz = _hyena_conv(taps, skip, hy3, 0, hy3, HY_W // cb, kf, tb, cb,
                         conv_u=True, u_off=0, g_off=HY_W, order=0)
        y_hy = _hyena_conv(taps, skip, zz, 0, hy3, 2 * HY_W // cb, kf, tb, cb,
                           conv_u=False, u_off=0, g_off=2 * HY_W, order=1)
        y_hy = y_hy.reshape(HY_W, nlat)
        if ctx_out:
            filt_c = _hyena_filters(CTX, *fargs)
            y_hy_c = _hyena_ctx(hyT, nlat // CTX, filt_c, taps2d, skip[:, None])
        else:
            y_hy_c = None

        d_exp = jnp.repeat(ssd_d[i], SSD_P)[None, :]
        yf, yb = _ssd(xbc, dt, ssd_conv_w[i], ssd_conv_b[i][None, :],
                      _pad_lanes(ssd_dt_bias[i].reshape(1, -1)), _pad_lanes(ssd_a_log[i].reshape(1, -1)),
                      d_exp, ex, nlat // CHUNK)

        qz, kk, v0, v1 = _attn_prep(qkv, cos_t, sin_t, jnp.tile(attn_q_norm[i], AT_H)[None, :],
                                    jnp.tile(attn_k_norm[i], AT_KV)[None, :], gsum)
        y_at = _flash_attention(qz, kk, v0, v1, nlat, 512, t)
        if ctx_out:
            y_at_c = _flash_attention(qz[:, :, nlat:], kk[nlat:], v0[nlt:], v1[nlt:], CTX, 128, CTX)
        else:
            y_at_c = None

        x1, h2 = _out_proj(y_hy, y_hy_c, yf, yb, z, y_at, y_at_c, x_main, x_ctx, g1, sc2, sh2,
                           w_out[i].astype(BF16), ssd_norm_w[i][None, :], ln1_g[i][None, :],
                           ln1_b[i][None, :], with_ctx=ctx_out, alpha=alpha)
        ntiles = nlt + 1 if ctx_out else nlt
        x_main = _conv_ffn(h2, x1, g2, ffn_w_up[i].astype(BF16), ffn_conv_w[i], ffn_conv_b[i][None, :],
                           ffn_w_down[i].astype(BF16), ln2_g[i][None, :], ln2_b[i][None, :],
                           nlat_tiles=nlt, ntiles=ntiles, alpha=alpha)
        x_ctx = None
    return x_main[None]
```

```python
import functools
import math

import numpy as np
import jax
import jax.numpy as jnp
from jax import lax
from jax.experimental import pallas as pl
from jax.experimental.pallas import tpu as pltpu

F32 = jnp.float32
BF16 = jnp.bfloat16
HI = lax.Precision.HIGHEST

D_MODEL = 1024
DEPTH = 2
GRID_W = 64
CTX = 256
EPS = 1e-6

HY_W = 256
HY_BANDS = 16
HY_EMB = 1 + 2 * HY_BANDS
HY_HID = 64
HY_MIN_DECAY = math.log(1e-2) / 1.5
HY_MAX_DECAY = math.log(1e-2) / 0.3
HY_COLS = 3 * HY_W

SSD_H = 8
SSD_P = 64
SSD_W = SSD_H * SSD_P
SSD_G = 2
SSD_R = SSD_H // SSD_G
SSD_N = 128
SSD_XBC = SSD_W + 2 * SSD_G * SSD_N
CHUNK = 128
SSD_BLK = 2 * CHUNK

AT_H = 4
AT_KV = 2
AT_D = 64
AT_W = AT_H * AT_D
AT_KVW = AT_KV * AT_D
ROPE_BASE = 10000.0

MIX_W = HY_W + SSD_W + AT_W
D_FF = 2816
FF_CHUNK = 256

TM = CTX
LANES = 128
VT_ROWS = AT_D + 16
N2 = 256
V7X_VMEM_LIMIT = 56 * 1024 * 1024


def _cparams(sem, vmem=None):
    return pltpu.CompilerParams(dimension_semantics=sem, vmem_limit_bytes=vmem)


def _dot(a, b):
    return jnp.dot(a.astype(BF16), b.astype(BF16), preferred_element_type=F32)


def _dot_hi(a, b):
    return jnp.dot(a, b, preferred_element_type=F32, precision=HI)


def _split(x, n):
    pieces = []
    for _ in range(n):
        p = x.astype(BF16)
        pieces.append(p)
        x = x - p.astype(F32)
    return pieces


def _sigmoid(x):
    return 1.0 / (1.0 + jnp.exp(-x))


def _silu(x):
    return x * _sigmoid(x)


def _full(shape):
    nd = len(shape)
    return pl.BlockSpec(shape, lambda *_: (0,) * nd)


def _mod_body(c_ref, w_ref, b_ref, o_ref):
    s = _silu(c_ref[...])
    o_ref[0] = _dot(s, w_ref[0]) + b_ref[0]


def _modulation(cvec, w_mod, b_mod):
    depth, d, n = w_mod.shape
    nb = 1536
    return pl.pallas_call(
        _mod_body,
        grid=(depth, n // nb),
        in_specs=[pl.BlockSpec((8, d), lambda l, j: (0, 0)),
                  pl.BlockSpec((1, d, nb), lambda l, j: (l, 0, j)),
                  pl.BlockSpec((1, 1, nb), lambda l, j: (l, 0, j))],
        out_specs=pl.BlockSpec((1, 8, nb), lambda l, j: (l, 0, j)),
        out_shape=jax.ShapeDtypeStruct((depth, 8, n), F32),
        compiler_params=_cparams(("arbitrary", "arbitrary")),
        name="modulation",
    )(cvec, w_mod, b_mod.reshape(depth, 1, n))


W_MAIN = SSD_W + SSD_XBC + LANES + AT_W + 2 * AT_KVW


def _inproj_body(*refs, split_x, nlat_tiles):
    if split_x:
        x_ref, xc_ref, *refs = refs
        x = jnp.where(pl.program_id(0) == nlat_tiles, xc_ref[...], x_ref[...])
    else:
        x_ref, *refs = refs
        x = x_ref[...]
    sc_ref, sh_ref, wm_ref, wh_ref, hy_ref, z_ref, xbc_ref, dt_ref, qkv_ref = refs
    h = (x * (1.0 + sc_ref[0]) + sh_ref[0]).astype(BF16)
    main = jnp.dot(h, wm_ref[...], preferred_element_type=F32)
    z_ref[...] = main[:, 0:SSD_W]
    xbc_ref[...] = main[:, SSD_W:SSD_W + SSD_XBC]
    o = SSD_W + SSD_XBC
    dt_ref[...] = main[:, o:o + LANES]
    qkv_ref[...] = main[:, o + LANES:]
    hy_ref[...] = lax.dot_general(wh_ref[...], h, (((1,), (1,)), ((), ())),
                                  preferred_element_type=F32)


def _in_proj(x_main, x_ctx, t, sc, sh, w_main, w_hyT):
    nt = t // TM
    sel = lambda i: (jnp.where(i == nt - 1, 1, 0), 0, 0)
    row = lambda i: (i, 0)
    split_x = x_ctx is not None
    x_specs = [pl.BlockSpec((TM, D_MODEL), lambda i: (jnp.minimum(i, nt - 2), 0)),
               _full((TM, D_MODEL))] if split_x else [pl.BlockSpec((TM, D_MODEL), row)]
    x_args = [x_main, x_ctx] if split_x else [x_main]
    return pl.pallas_call(
        functools.partial(_inproj_body, split_x=split_x, nlat_tiles=nt - 1),
        grid=(nt,),
        in_specs=x_specs + [
                  pl.BlockSpec((1, 1, D_MODEL), sel),
                  pl.BlockSpec((1, 1, D_MODEL), sel),
                  _full((D_MODEL, W_MAIN)),
                  _full((HY_COLS, D_MODEL))],
        out_specs=[pl.BlockSpec((HY_COLS, TM), lambda i: (0, i)),
                   pl.BlockSpec((TM, SSD_W), row),
                   pl.BlockSpec((TM, SSD_XBC), row),
                   pl.BlockSpec((TM, LANES), row),
                   pl.BlockSpec((TM, AT_W + 2 * AT_KVW), row)],
        out_shape=[jax.ShapeDtypeStruct((HY_COLS, t), F32),
                   jax.ShapeDtypeStruct((t, SSD_W), F32),
                   jax.ShapeDtypeStruct((t, SSD_XBC), F32),
                   jax.ShapeDtypeStruct((t, LANES), F32),
                   jax.ShapeDtypeStruct((t, AT_W + 2 * AT_KVW), F32)],
        compiler_params=_cparams(("arbitrary",), V7X_VMEM_LIMIT),
        name="in_proj",
    )(*x_args, sc, sh, w_main, w_hyT)


def _filter_body(w1_ref, b1_ref, w2_ref, b2_ref, w3_ref, fr_ref, o_ref, *, n, nt):
    j = pl.program_id(0)
    k = (lax.broadcasted_iota(jnp.int32, (1, nt), 1) + j * nt).astype(F32)
    t = k * (1.0 / (n - 1))
    wk = k * (2.0 * math.pi / n)
    band = lax.broadcasted_iota(jnp.int32, (HY_BANDS, nt), 0).astype(F32)
    arg = (1e-4 + band * ((HY_BANDS - 1 - 1e-4) / (HY_BANDS - 1))) * wk
    r8 = lax.broadcasted_iota(jnp.int32, (8, nt), 0)
    feats = jnp.concatenate([jnp.cos(arg), -jnp.sin(arg), jnp.where(r8 == 0, t, 0.0),
                             jnp.zeros((LANES - 2 * HY_BANDS - 8, nt), F32)], axis=0)
    fr = fr_ref[...]
    h = jnp.sin(fr * (_dot_hi(w1_ref[...], feats) + b1_ref[...]))
    h = jnp.sin(fr * (_dot_hi(w2_ref[...], h) + b2_ref[...]))
    h = _dot(w3_ref[...], h)
    c = lax.broadcasted_iota(jnp.int32, (HY_W, nt), 0).astype(F32)
    delta = jnp.abs(HY_MIN_DECAY + c * ((HY_MAX_DECAY - HY_MIN_DECAY) / (HY_W - 1)))
    window = jnp.exp(-t * delta)
    for q in range(4):
        o_ref[q] = h[q * HY_W:(q + 1) * HY_W] * window


def _hyena_filters(n, w1t, b1, w2t, b2, w3t, freq):
    nt = min(n, 2048)
    return pl.pallas_call(
        functools.partial(_filter_body, n=n, nt=nt),
        grid=(n // nt,),
        in_specs=[_full((HY_HID, LANES)), _full((HY_HID, 1)), _full((HY_HID, HY_HID)),
                  _full((HY_HID, 1)), _full((4 * HY_W, HY_HID)), _full((HY_HID, 1))],
        out_specs=pl.BlockSpec((4, HY_W, nt), lambda j: (0, 0, j)),
        out_shape=jax.ShapeDtypeStruct((4, HY_W, n), F32),
        compiler_params=_cparams(("arbitrary",)),
        name="hyena_filters",
    )(w1t, b1, w2t, b2, w3t, freq)


def _fft_tables(n):
    big = 2 * n
    n1 = big // N2
    r1 = n1 // 2
    kp = -(-(r1 + 1) // 8) * 8
    k1 = np.arange(kp)[:, None].astype(np.float64)
    live = (k1 <= r1)
    a1 = 2 * np.pi * k1 * np.arange(r1)[None, :] / n1
    f1a = np.concatenate([np.cos(a1) * live, -np.sin(a1) * live], axis=0)
    at = 2 * np.pi * k1 * np.arange(N2)[None, :] / big
    twr, twi = np.cos(at) * live, -np.sin(at) * live
    a2 = 2 * np.pi * np.outer(np.arange(N2), np.arange(N2)) / N2
    c2, s2 = np.cos(a2), np.sin(a2)
    w2f = np.block([[c2, -s2], [s2, c2]])
    w2i = np.block([[c2, s2], [-s2, c2]])
    wgt = np.where((k1 == 0) | (k1 == r1), 1.0, 2.0) * live / big
    f1i = np.concatenate([(np.cos(a1) * wgt).T, (-np.sin(a1) * wgt).T], axis=1)
    j = jnp.asarray
    return dict(r1=r1, kp=kp, f1a=j(f1a, BF16), twr=j(twr, F32), twi=j(twi, F32),
                w2f=j(w2f, BF16), w2i=j(w2i, BF16), f1i=j(f1i, BF16))


def _fft_rows(x, f1a_ref, twr_ref, twi_ref, kp):
    a = jnp.dot(f1a_ref[...], x.astype(BF16), preferred_element_type=F32)
    ar, ai = a[:kp], a[kp:]
    twr, twi = twr_ref[...], twi_ref[...]
    return jnp.concatenate([ar * twr - ai * twi, ar * twi + ai * twr], axis=1)


def _kspec_body(hf_ref, hb_ref, f1a_ref, twr_ref, twi_ref, w2f_ref, o_ref, sa_ref, *, cb, kp, r1):
    row = lax.broadcasted_iota(jnp.int32, (r1, N2), 0)
    lane = lax.broadcasted_iota(jnp.int32, (r1, N2), 1)
    first = (row == 0) & (lane == 0)

    for c in range(cb):
        af = _fft_rows(hf_ref[0, c], f1a_ref, twr_ref, twi_ref, kp)
        ab = _fft_rows(jnp.where(first, 0.0, hb_ref[0, c]), f1a_ref, twr_ref, twi_ref, kp)
        sa_ref[c * kp:(c + 1) * kp, :] = af + ab
        sa_ref[(cb + c) * kp:(cb + c + 1) * kp, :] = af - ab
    kr = jnp.dot(sa_ref[0:cb * kp, :].astype(BF16), w2f_ref[:, 0:N2], preferred_element_type=F32)
    ki = jnp.dot(sa_ref[cb * kp:, :].astype(BF16), w2f_ref[:, N2:], preferred_element_type=F32)
    o_ref[0] = jnp.concatenate([kr, ki], axis=1).reshape(cb, kp, 2 * N2)


def _filter_spectra(filt, tb, cb):
    r1, kp = tb["r1"], tb["kp"]
    return pl.pallas_call(
        functools.partial(_kspec_body, cb=cb, kp=kp, r1=r1),
        grid=(2, HY_W // cb),
        in_specs=[pl.BlockSpec((1, cb, r1, N2), lambda o, j: (2 * o, j, 0, 0)),
                  pl.BlockSpec((1, cb, r1, N2), lambda o, j: (2 * o + 1, j, 0, 0)),
                  _full((2 * kp, r1)), _full((kp, N2)), _full((kp, N2)), _full((2 * N2, 2 * N2))],
        out_specs=pl.BlockSpec((1, cb, kp, 2 * N2), lambda o, j: (o, j, 0, 0)),
        out_shape=jax.ShapeDtypeStruct((2, HY_W, kp, 2 * N2), F32),
        scratch_shapes=[pltpu.VMEM((2 * cb * kp, 2 * N2), F32)],
        compiler_params=_cparams(("arbitrary", "arbitrary"), V7X_VMEM_LIMIT),
        name="hyena_filter_spectra",
    )(filt, filt, tb["f1a"], tb["twr"], tb["twi"], tb["w2f"])


def _short_conv(x, w0, w1, w2, b):
    r1 = x.shape[0]
    row = lax.broadcasted_iota(jnp.int32, x.shape, 0)
    lane = lax.broadcasted_iota(jnp.int32, x.shape, 1)
    a = pltpu.roll(x, 1, 1)
    prev = jnp.where(lane == 0, jnp.where(row == 0, 0.0, pltpu.roll(a, 1, 0)), a)
    a = pltpu.roll(x, N2 - 1, 1)
    nxt = jnp.where(lane == N2 - 1, jnp.where(row == r1 - 1, 0.0, pltpu.roll(a, r1 - 1, 0)), a)
    return w0 * prev + w1 * x + w2 * nxt + b


def _hyconv_body(taps_ref, skip_ref, u_ref, g_ref, kf_ref, f1a_ref, twr_ref, twi_ref, w2f_ref,
                 w2i_ref, f1i_ref, o_ref, sa_ref, sq_ref, sx_ref, *, cb, kp, conv_u, u_off,
                 g_off, order):
    j = pl.program_id(0)

    def taps(ch):
        return (taps_ref[ch * 4], taps_ref[ch * 4 + 1], taps_ref[ch * 4 + 2], taps_ref[ch * 4 + 3])

    for c in range(cb):
        x = u_ref[c]
        if conv_u:
            x = _short_conv(x, *taps(u_off + j * cb + c))
        sx_ref[c] = x
        sa_ref[c * kp:(c + 1) * kp, :] = _fft_rows(x, f1a_ref, twr_ref, twi_ref, kp)
    x = jnp.dot(sa_ref[...].astype(BF16), w2f_ref[...], preferred_element_type=F32)
    k = kf_ref[0].reshape(cb * kp, 2 * N2)
    xr, xi, kr, ki = x[:, :N2], x[:, N2:], k[:, :N2], k[:, N2:]
    p = jnp.concatenate([xr * kr - xi * ki, xr * ki + xi * kr], axis=1).astype(BF16)
    sq_ref[...] = jnp.dot(p, w2i_ref[...], preferred_element_type=F32)

    for c in range(cb):
        q = sq_ref[c * kp:(c + 1) * kp, :]
        qr, qi = q[:, :N2], q[:, N2:]
        twr, twi = twr_ref[...], twi_ref[...]
        y2 = jnp.concatenate([qr * twr + qi * twi, qi * twr - qr * twi], axis=0).astype(BF16)
        y = jnp.dot(f1i_ref[...], y2, preferred_element_type=F32)
        y = y + skip_ref[order * HY_W + j * cb + c] * sx_ref[c]
        o_ref[c] = _short_conv(g_ref[c], *taps(g_off + j * cb + c)) * y


def _hyena_conv(taps, skip, u3, u_blk0, g3, g_blk0, kf, tb, cb, *, conv_u, u_off, g_off, order):
    r1, kp = tb["r1"], tb["kp"]
    smem = pl.BlockSpec(memory_space=pltpu.SMEM)
    return pl.pallas_call(
        functools.partial(_hyconv_body, cb=cb, kp=kp, conv_u=conv_u, u_off=u_off, g_off=g_off,
                          order=order),
        grid=(HY_W // cb,),
        in_specs=[smem, smem,
                  pl.BlockSpec((cb, r1, N2), lambda j: (j + u_blk0, 0, 0)),
                  pl.BlockSpec((cb, r1, N2), lambda j: (j + g_blk0, 0, 0)),
                  pl.BlockSpec((1, cb, kp, 2 * N2), lambda j: (order, j, 0, 0)),
                  _full((2 * kp, r1)), _full((kp, N2)), _full((kp, N2)),
                  _full((2 * N2, 2 * N2)), _full((2 * N2, 2 * N2)), _full((r1, 2 * kp))],
        out_specs=pl.BlockSpec((cb, r1, N2), lambda j: (j, 0, 0)),
        out_shape=jax.ShapeDtypeStruct((HY_W, r1, N2), F32),
        scratch_shapes=[pltpu.VMEM((cb * kp, 2 * N2), F32),
                        pltpu.VMEM((cb * kp, 2 * N2), F32),
                        pltpu.VMEM((cb, r1, N2), F32)],
        compiler_params=_cparams(("arbitrary",), V7X_VMEM_LIMIT),
        name="hyena_conv%d" % order,
    )(taps, skip, u3, g3, kf, tb["f1a"], tb["twr"], tb["twi"], tb["w2f"], tb["w2i"], tb["f1i"])


def _ctx_tables(n):
    big = 2 * n
    a = 2 * np.pi * np.outer(np.arange(n), np.arange(big)) / big
    fc = np.concatenate([np.cos(a), -np.sin(a)], axis=1)
    gi = np.concatenate([np.cos(a).T, -np.sin(a).T], axis=0) / big
    return jnp.asarray(fc, BF16), jnp.asarray(gi, BF16)


def _hyctx_body(hy_ref, filt_ref, taps_ref, skip_ref, fc_ref, gi_ref, o_ref, *, n):
    big = 2 * n
    lane = lax.broadcasted_iota(jnp.int32, (HY_W, n), 1)

    def sconv(x, t):
        prev = jnp.where(lane == 0, 0.0, pltpu.roll(x, 1, 1))
        nxt = jnp.where(lane == n - 1, 0.0, pltpu.roll(x, n - 1, 1))
        return t[:, 0:1] * prev + t[:, 1:2] * x + t[:, 2:3] * nxt + t[:, 3:4]

    def spec(o):
        xf = _dot(filt_ref[2 * o], fc_ref[...])
        xb = _dot(jnp.where(lane == 0, 0.0, filt_ref[2 * o + 1]), fc_ref[...])
        return xf[:, :big] + xb[:, :big], xf[:, big:] - xb[:, big:]

    def lconv(u, o):
        kr, ki = spec(o)
        x = _dot(u, fc_ref[...])
        xr, xi = x[:, :big], x[:, big:]
        p = jnp.concatenate([xr * kr - xi * ki, xr * ki + xi * kr], axis=1)
        return _dot(p, gi_ref[...]) + skip_ref[o * HY_W:(o + 1) * HY_W, :] * u

    v = sconv(hy_ref[0:HY_W, :], taps_ref[0:HY_W, :])
    x1 = sconv(hy_ref[HY_W:2 * HY_W, :], taps_ref[HY_W:2 * HY_W, :])
    x2 = sconv(hy_ref[2 * HY_W:, :], taps_ref[2 * HY_W:, :])
    z = x1 * lconv(v, 0)
    o_ref[...] = x2 * lconv(z, 1)


def _hyena_ctx(hyT, blk, filt_ctx, taps2d, skip2d):
    n = filt_ctx.shape[-1]
    fc, gi = _ctx_tables(n)
    return pl.pallas_call(
        functools.partial(_hyctx_body, n=n),
        grid=(1,),
        in_specs=[pl.BlockSpec((HY_COLS, n), lambda i: (0, blk)),
                  _full((4, HY_W, n)), _full((HY_COLS, 4)), _full((2 * HY_W, 1)),
                  _full((n, 4 * n)), _full((4 * n, n))],
        out_specs=_full((HY_W, n)),
        out_shape=jax.ShapeDtypeStruct((HY_W, n), F32),
        compiler_params=_cparams(("arbitrary",), V7X_VMEM_LIMIT),
        name="hyena_ctx",
    )(hyT, filt_ctx, taps2d, skip2d, fc, gi)


def _ssd_body(xf_ref, xfp_ref, xfn_ref, dtf_ref, xb_ref, xbp_ref, xbn_ref, dtb_ref,
              cw_ref, cb_ref, dtbias_ref, alog_ref, dexp_ref, ex_ref,
              yf_ref, yb_ref, h_ref, sx_ref, *, nlat):
    j = pl.program_id(0)
    nblk = nlat + 1

    @pl.when(j == 0)
    def _():
        h_ref[...] = jnp.zeros_like(h_ref)

    ri = lax.broadcasted_iota(jnp.int32, (CHUNK, CHUNK), 0)
    ci = lax.broadcasted_iota(jnp.int32, (CHUNK, CHUNK), 1)
    a_all = -jnp.exp(alog_ref[...])

    for d in range(2):
        x_ref, xp_ref, xn_ref, dt_ref, y_ref = (
            (xf_ref, xfp_ref, xfn_ref, dtf_ref, yf_ref) if d == 0
            else (xb_ref, xbp_ref, xbn_ref, dtb_ref, yb_ref))
        bid = jnp.where(j < 1, nlat, j - 1) if d == 0 else nblk - 1 - j
        pv = ((bid != 0) & (bid != nlat)).astype(F32)
        nv = ((bid != nlat - 1) & (bid != nblk - 1)).astype(F32)
        sx_ref[0:8, :] = xp_ref[...] * pv
        sx_ref[8:8 + SSD_BLK, :] = x_ref[...]
        sx_ref[8 + SSD_BLK:, :] = xn_ref[...] * nv
        pre = (cw_ref[0:1, :] * sx_ref[pl.ds(7, SSD_BLK), :] + cw_ref[1:2, :] * x_ref[...]
               + cw_ref[2:3, :] * sx_ref[pl.ds(9, SSD_BLK), :] + cb_ref[...])
        xc_blk = _silu(pre)
        z = dt_ref[...] + dtbias_ref[...]
        dtv_blk = jnp.maximum(z, 0.0) + jnp.log(1.0 + jnp.exp(-jnp.abs(z)))
        y_parts = [None, None]
        for sub in ((0, 1) if d == 0 else (1, 0)):
            y_parts[sub] = _ssd_chunk(d, xc_blk[sub * CHUNK:(sub + 1) * CHUNK],
                                      dtv_blk[sub * CHUNK:(sub + 1) * CHUNK], a_all, ri, ci,
                                      dexp_ref, ex_ref, h_ref)
        y_ref[...] = jnp.concatenate(y_parts, axis=0)


def _ssd_chunk(d, xc, dtv, a_all, ri, ci, dexp_ref, ex_ref, h_ref):
    xs = xc[:, 0:SSD_W]
    da = dtv * a_all
    tri = (ri >= ci) if d == 0 else (ri <= ci)
    r3 = jnp.dot(tri.astype(BF16), jnp.concatenate(_split(da, 3), axis=1),
                 preferred_element_type=F32)
    acum = r3[:, 0:LANES] + r3[:, LANES:2 * LANES] + r3[:, 2 * LANES:]
    last = CHUNK - 1 if d == 0 else 0
    acum_t = acum.T
    pieces = jnp.concatenate(_split(acum, 2) + _split(dtv, 2), axis=1)
    ys = []
    for g in range(SSD_G):
        bm = xc[:, SSD_W + g * SSD_N:SSD_W + (g + 1) * SSD_N]
        cm = xc[:, SSD_W + (SSD_G + g) * SSD_N:SSD_W + (SSD_G + g + 1) * SSD_N]
        bt = bm.T
        scores = _dot(cm, bt)
        both = jnp.dot(pieces, ex_ref[d * SSD_G + g], preferred_element_type=F32)
        acum_e, dt_e = both[:, 0:SSD_R * SSD_P], both[:, SSD_R * SSD_P:]
        tot_e = acum_e[last:last + 1, :]
        xg = xs[:, g * SSD_R * SSD_P:(g + 1) * SSD_R * SSD_P]
        hg = h_ref[d, g]
        yoff = _dot(cm, hg) * jnp.exp(acum_e)
        ydiag = []
        for r in range(SSD_R):
            hl = d * SSD_H + g * SSD_R + r
            seg = acum[:, hl:hl + 1] - acum_t[hl:hl + 1, :]
            m = scores * jnp.exp(jnp.where(tri, seg, -jnp.inf))
            xdt = xg[:, r * SSD_P:(r + 1) * SSD_P] * dt_e[:, r * SSD_P:(r + 1) * SSD_P]
            ydiag.append(_dot(m, xdt))
        ys.append(jnp.concatenate(ydiag, axis=1) + yoff)
        xw = xg * (jnp.exp(tot_e - acum_e) * dt_e)
        h_ref[d, g] = hg * jnp.exp(tot_e) + _dot(bt, xw)
    y = jnp.concatenate(ys, axis=1)
    if d == 0:
        y = y + dexp_ref[...] * xs
    return y


def _ssd(xbc, dt, conv_w, conv_b, dt_bias, a_log, d_exp, ex, nlat):
    t = xbc.shape[0]
    nblk = t // SSD_BLK
    hb = SSD_BLK // 8
    last8 = t // 8 - 1
    fc = lambda j: jnp.where(j < 1, nlat, j - 1)
    bc = lambda j: nblk - 1 - j
    prev = lambda f: (lambda j: (jnp.maximum(f(j) * hb - 1, 0), 0))
    nxt = lambda f: (lambda j: (jnp.minimum((f(j) + 1) * hb, last8), 0))
    main = lambda f: (lambda j: (f(j), 0))
    return pl.pallas_call(
        functools.partial(_ssd_body, nlat=nlat),
        grid=(nblk,),
        in_specs=[pl.BlockSpec((SSD_BLK, SSD_XBC), main(fc)), pl.BlockSpec((8, SSD_XBC), prev(fc)),
                  pl.BlockSpec((8, SSD_XBC), nxt(fc)), pl.BlockSpec((SSD_BLK, LANES), main(fc)),
                  pl.BlockSpec((SSD_BLK, SSD_XBC), main(bc)), pl.BlockSpec((8, SSD_XBC), prev(bc)),
                  pl.BlockSpec((8, SSD_XBC), nxt(bc)), pl.BlockSpec((SSD_BLK, LANES), main(bc)),
                  _full((3, SSD_XBC)), _full((1, SSD_XBC)), _full((1, LANES)), _full((1, LANES)),
                  _full((1, SSD_W)), _full((2 * SSD_G, 4 * LANES, 2 * SSD_R * SSD_P))],
        out_specs=[pl.BlockSpec((SSD_BLK, SSD_W), main(fc)), pl.BlockSpec((SSD_BLK, SSD_W), main(bc))],
        out_shape=[jax.ShapeDtypeStruct((t, SSD_W), F32), jax.ShapeDtypeStruct((t, SSD_W), F32)],
        scratch_shapes=[pltpu.VMEM((2, SSD_G, SSD_N, SSD_R * SSD_P), F32),
                        pltpu.VMEM((SSD_BLK + 16, SSD_XBC), F32)],
        compiler_params=_cparams(("arbitrary",), V7X_VMEM_LIMIT),
        name="ssd_scan",
    )(xbc, xbc, xbc, dt, xbc, xbc, xbc, dt, conv_w, conv_b, dt_bias, a_log, d_exp, ex)


def _attn_prep_body(qkv_ref, cos_ref, sin_ref, qn_ref, kn_ref, gsum_ref, q_ref, k_ref, v0_ref, v1_ref):
    x = qkv_ref[...]
    qk = x[:, 0:AT_W + AT_KVW]
    ms = jnp.dot(jnp.concatenate(_split(qk * qk, 2), axis=1), gsum_ref[...],
                 preferred_element_type=F32)
    wn = jnp.concatenate([qn_ref[...], kn_ref[...]], axis=1)
    qk = qk * lax.rsqrt(ms + EPS) * wn
    w = qk.shape[1]
    lane = lax.broadcasted_iota(jnp.int32, qk.shape, 1)
    partner = jnp.where(lane % AT_D < AT_D // 2, pltpu.roll(qk, w - AT_D // 2, 1),
                        pltpu.roll(qk, AT_D // 2, 1))
    cos = jnp.concatenate([cos_ref[...]] * 3, axis=1)
    sin = jnp.concatenate([sin_ref[...]] * 3, axis=1)
    qk = qk * cos + partner * sin
    q = qk[:, 0:AT_W] * (AT_D ** -0.5 * math.log2(math.e))
    zeros = jnp.zeros((q.shape[0], AT_D), F32)
    for h in range(AT_H):
        qh = q[:, h * AT_D:(h + 1) * AT_D]
        qh = (jnp.concatenate([qh, zeros], axis=1) if h // 2 == 0
              else jnp.concatenate([zeros, qh], axis=1))
        q_ref[h] = qh.T.astype(BF16)
    k_ref[...] = qk[:, AT_W:].astype(BF16)
    v = x[:, AT_W + AT_KVW:]
    vl = lax.broadcasted_iota(jnp.int32, v.shape, 1)
    for g, v_ref in enumerate((v0_ref, v1_ref)):
        vg = v if g == 0 else pltpu.roll(v, AT_D, 1)
        vg = jnp.where(vl < AT_D, vg, jnp.where(vl == AT_D, 1.0, 0.0))
        v_ref[0] = vg.T[0:VT_ROWS].astype(BF16)


def _attn_prep(qkv, cos_t, sin_t, qn, kn, gsum):
    t = qkv.shape[0]
    row = lambda i: (i, 0)
    return pl.pallas_call(
        _attn_prep_body,
        grid=(t // TM,),
        in_specs=[pl.BlockSpec((TM, AT_W + 2 * AT_KVW), row), pl.BlockSpec((TM, LANES), row),
                  pl.BlockSpec((TM, LANES), row), _full((1, AT_W)), _full((1, AT_KVW)),
                  _full((2 * (AT_W + AT_KVW), AT_W + AT_KVW))],
        out_specs=[pl.BlockSpec((AT_H, LANES, TM), lambda i: (0, 0, i)),
                   pl.BlockSpec((TM, LANES), row),
                   pl.BlockSpec((1, VT_ROWS, TM), lambda i: (i, 0, 0)),
                   pl.BlockSpec((1, VT_ROWS, TM), lambda i: (i, 0, 0))],
        out_shape=[jax.ShapeDtypeStruct((AT_H, LANES, t), BF16),
                   jax.ShapeDtypeStruct((t, LANES), BF16),
                   jax.ShapeDtypeStruct((t // TM, VT_ROWS, TM), BF16),
                   jax.ShapeDtypeStruct((t // TM, VT_ROWS, TM), BF16)],
        compiler_params=_cparams(("arbitrary",)),
        name="attn_prep",
    )(qkv, cos_t, sin_t, qn, kn, gsum)


def _flash_body(q_ref, k_ref, v0_ref, v1_ref, o_ref, s_ref, *, tq, kc, nkc):
    v_refs = (v0_ref, v1_ref)
    nb = kc // TM
    qs = [jnp.concatenate([q_ref[2 * g], q_ref[2 * g + 1]], axis=1) for g in range(AT_KV)]

    def scores(c, slot):
        k = k_ref[pl.ds(pl.multiple_of(c * kc, kc), kc), :]
        for g in range(AT_KV):
            s_ref[slot, g] = jnp.dot(k, qs[g], preferred_element_type=F32)

    def update(c, slot, carry):
        out = []
        for g, (m, acc) in enumerate(carry):
            s = s_ref[slot, g]
            m_new = jnp.maximum(m, jnp.max(s, axis=0, keepdims=True))
            p = jnp.exp2(s - m_new).astype(BF16)
            pv = jnp.dot(v_refs[g][c * nb], p[0:TM], preferred_element_type=F32)
            for b in range(1, nb):
                pv = pv + jnp.dot(v_refs[g][c * nb + b], p[b * TM:(b + 1) * TM],
                                  preferred_element_type=F32)
            out.append((m_new, jnp.exp2(m - m_new) * acc + pv))
        return tuple(out)

    def pair(i, carry):
        scores(2 * i + 1, 1)
        carry = update(2 * i, 0, carry)
        scores(2 * i + 2, 0)
        return update(2 * i + 1, 1, carry)

    scores(0, 0)
    carry = tuple((jnp.full((1, 2 * tq), -jnp.inf, F32), jnp.zeros((VT_ROWS, 2 * tq), F32))
                  for _ in range(AT_KV))
    carry = lax.fori_loop(0, (nkc - 1) // 2, pair, carry)
    if (nkc - 1) % 2:
        scores(nkc - 1, 1)
        carry = update(nkc - 2, 0, carry)
        carry = update(nkc - 1, 1, carry)
    else:
        carry = update(nkc - 1, 0, carry)
    lane = lax.broadcasted_iota(jnp.int32, (tq, LANES), 1)
    outs = []
    for _, acc in carry:
        o = acc[0:AT_D] / acc[AT_D:AT_D + 1, :]
        o = jnp.concatenate([o, jnp.zeros((LANES - AT_D, 2 * tq), F32)], axis=0).T
        outs.append(jnp.where(lane < AT_D, o[:tq], pltpu.roll(o[tq:], AT_D, 1)))
    o_ref[...] = jnp.concatenate(outs, axis=1)


def _flash_attention(qt, k, v0t, v1t, nq_rows, tq, kv_rows, q_blk0=0):
    nd = kv_rows // TM
    kc = TM * max(d for d in range(1, 6) if nd % d == 0)
    kspec = pl.BlockSpec((kv_rows, LANES), lambda i: (0, 0))
    vspec = pl.BlockSpec((nd, VT_ROWS, TM), lambda i: (0, 0, 0))
    return pl.pallas_call(
        functools.partial(_flash_body, tq=tq, kc=kc, nkc=kv_rows // kc),
        grid=(nq_rows // tq,),
        in_specs=[pl.BlockSpec((AT_H, LANES, tq), lambda i: (0, 0, i + q_blk0)),
                  kspec, vspec, vspec],
        out_specs=pl.BlockSpec((tq, AT_W), lambda i: (i, 0)),
        out_shape=jax.ShapeDtypeStruct((nq_rows, AT_W), F32),
        scratch_shapes=[pltpu.VMEM((2, AT_KV, kc, 2 * tq), F32)],
        compiler_params=_cparams(("arbitrary",), V7X_VMEM_LIMIT),
        name="flash_attention",
    )(qt, k, v0t, v1t)


def _layernorm(x, g, b):
    mu = jnp.mean(x, axis=-1, keepdims=True)
    xc = x - mu
    var = jnp.mean(xc * xc, axis=-1, keepdims=True)
    return xc * lax.rsqrt(var + EPS) * g + b


def _outproj_body(*refs, with_ctx, split_x, nlat_tiles, alpha):
    refs = list(refs)
    take = lambda n: [refs.pop(0) for _ in range(n)]
    hyl_ref, = take(1)
    hyc_ref, = take(1) if with_ctx else (None,)
    yf_ref, yb_ref, z_ref, atl_ref = take(4)
    atc_ref, = take(1) if with_ctx else (None,)
    x_ref, = take(1)
    xc_ref, = take(1) if split_x else (None,)
    g1_ref, sc2_ref, sh2_ref, w_ref, nw_ref, lg_ref, lb_ref, x1_ref, h2_ref = refs
    hy, at, x = hyl_ref[...], atl_ref[...], x_ref[...]
    if with_ctx:
        is_ctx = pl.program_id(0) == nlat_tiles
        hy = jnp.where(is_ctx, hyc_ref[...], hy)
        at = jnp.where(is_ctx, atc_ref[...], at)
        if split_x:
            x = jnp.where(is_ctx, xc_ref[...], x)
    m = _dot(hy.T, w_ref[0:HY_W, :])
    gated = (yf_ref[...] + yb_ref[...]) * _silu(z_ref[...])
    gw = SSD_W // SSD_G
    parts = []
    for g in range(SSD_G):
        gg = gated[:, g * gw:(g + 1) * gw]
        parts.append(gg * lax.rsqrt(jnp.mean(gg * gg, axis=-1, keepdims=True) + EPS))
    ssd = jnp.concatenate(parts, axis=1) * nw_ref[...]
    m = m + _dot(ssd, w_ref[HY_W:HY_W + SSD_W, :]) + _dot(at, w_ref[HY_W + SSD_W:, :])
    x1 = _layernorm(alpha * x + g1_ref[0] * m, lg_ref[...], lb_ref[...])
    x1_ref[...] = x1
    h2_ref[...] = (x1 * (1.0 + sc2_ref[0]) + sh2_ref[0]).astype(BF16)


def _out_proj(hy_lat, hy_ctx, yf, yb, z, at_lat, at_ctx, x_main, x_ctx, g1, sc2, sh2, w_out, nw, lg, lb,
              *, with_ctx, alpha):
    t = yf.shape[0]
    split_x = with_ctx and x_ctx is not None
    nlat_tiles = hy_lat.shape[1] // TM
    nt = t // TM if with_ctx else nlat_tiles
    last = t // TM - 1
    sel = lambda i: (jnp.where(i == last, 1, 0), 0, 0)
    row = lambda i: (i, 0)
    lat_row = lambda i: (jnp.minimum(i, nlat_tiles - 1), 0)
    lat_col = lambda i: (0, jnp.minimum(i, nlat_tiles - 1))
    vec = pl.BlockSpec((1, 1, D_MODEL), sel)
    specs = [pl.BlockSpec((HY_W, TM), lat_col)]
    args = [hy_lat]
    if with_ctx:
        specs.append(_full((HY_W, TM)))
        args.append(hy_ctx)
    specs += [pl.BlockSpec((TM, SSD_W), row)] * 3 + [pl.BlockSpec((TM, AT_W), lat_row)]
    args += [yf, yb, z, at_lat]
    if with_ctx:
        specs.append(_full((TM, AT_W)))
        args.append(at_ctx)
    specs.append(pl.BlockSpec((TM, D_MODEL), lat_row if x_ctx is not None else row))
    args.append(x_main)
    if split_x:
        specs.append(_full((TM, D_MODEL)))
        args.append(x_ctx)
    specs += [vec, vec, vec, _full((MIX_W, D_MODEL)),
              _full((1, SSD_W)), _full((1, D_MODEL)), _full((1, D_MODEL))]
    args += [g1, sc2, sh2, w_out, nw, lg, lb]
    return pl.pallas_call(
        functools.partial(_outproj_body, with_ctx=with_ctx, split_x=split_x, nlat_tiles=nlat_tiles,
                          alpha=alpha),
        grid=(nt,),
        in_specs=specs,
        out_specs=[pl.BlockSpec((TM, D_MODEL), row), pl.BlockSpec((TM, D_MODEL), row)],
        out_shape=[jax.ShapeDtypeStruct((nt * TM, D_MODEL), F32),
                   jax.ShapeDtypeStruct((nt * TM, D_MODEL), BF16)],
        compiler_params=_cparams(("arbitrary",), V7X_VMEM_LIMIT),
        name="out_proj",
    )(*args)


HALO = 16


def _ffn_body(h_ref, hp_ref, hn_ref, x1_ref, g2_ref, wup_ref, cw_ref, cb_ref, wdn_ref, lg_ref, lb_ref,
              o_ref, *, nlat_tiles, ntiles, alpha):
    i = pl.program_id(0)
    pv = (i != 0) & (i != nlat_tiles)
    nv = (i != nlat_tiles - 1) & (i != ntiles - 1)
    zero = jnp.zeros((HALO, D_MODEL), BF16)
    hp = jnp.where(pv, hp_ref[...], zero)
    hn = jnp.where(nv, hn_ref[...], zero)
    hext = jnp.concatenate([hp, h_ref[...], hn], axis=0)
    acc = jnp.zeros((TM, D_MODEL), F32)
    nchunks = D_FF // FF_CHUNK

    def up(c):
        return [jnp.dot(hext, wup_ref[:, o:o + FF_CHUNK], preferred_element_type=F32)
                for o in (c * FF_CHUNK, D_FF + c * FF_CHUNK)]

    def conv(u, o):
        cw = cw_ref[:, o:o + FF_CHUNK]
        prev = pltpu.roll(u, 1, 0)[HALO:HALO + TM]
        nxt = pltpu.roll(u, TM + 2 * HALO - 1, 0)[HALO:HALO + TM]
        return (cw[0:1] * prev + cw[1:2] * u[HALO:HALO + TM] + cw[2:3] * nxt
                + cb_ref[:, o:o + FF_CHUNK])

    ahead = 2
    queue = [up(c) for c in range(ahead)]
    for c in range(nchunks):
        if c + ahead < nchunks:
            queue.append(up(c + ahead))
        us = queue.pop(0)
        c0 = c * FF_CHUNK
        act = (_silu(conv(us[0], c0)) * conv(us[1], D_FF + c0)).astype(BF16)
        acc = acc + jnp.dot(act, wdn_ref[c0:c0 + FF_CHUNK, :], preferred_element_type=F32)
    o_ref[...] = _layernorm(alpha * x1_ref[...] + g2_ref[0] * acc, lg_ref[...], lb_ref[...])


def _conv_ffn(h2, x1, g2, w_up, conv_w, conv_b, w_down, lg, lb, *, nlat_tiles, ntiles, alpha):
    hb = TM // HALO
    lastb = ntiles * hb - 1
    row = lambda i: (i, 0)
    sel = lambda i: (jnp.where(i == nlat_tiles, 1, 0), 0, 0)
    return pl.pallas_call(
        functools.partial(_ffn_body, nlat_tiles=nlat_tiles, ntiles=ntiles, alpha=alpha),
        grid=(ntiles,),
        in_specs=[pl.BlockSpec((TM, D_MODEL), row),
                  pl.BlockSpec((HALO, D_MODEL), lambda i: (jnp.maximum(i * hb - 1, 0), 0)),
                  pl.BlockSpec((HALO, D_MODEL), lambda i: (jnp.minimum((i + 1) * hb, lastb), 0)),
                  pl.BlockSpec((TM, D_MODEL), row),
                  pl.BlockSpec((1, 1, D_MODEL), sel),
                  _full((D_MODEL, 2 * D_FF)), _full((3, 2 * D_FF)), _full((1, 2 * D_FF)),
                  _full((D_FF, D_MODEL)), _full((1, D_MODEL)), _full((1, D_MODEL))],
        out_specs=pl.BlockSpec((TM, D_MODEL), row),
        out_shape=jax.ShapeDtypeStruct((ntiles * TM, D_MODEL), F32),
        compiler_params=_cparams(("arbitrary",), V7X_VMEM_LIMIT),
        name="conv_ffn",
    )(h2, h2, h2, x1, g2, w_up, conv_w, conv_b, w_down, lg, lb)


def _rope_tables(nlat):
    rows = nlat // GRID_W
    row = jnp.repeat(jnp.arange(rows, dtype=F32), GRID_W)
    col = jnp.tile(jnp.arange(GRID_W, dtype=F32), rows)
    nf = AT_D // 4
    inv = ROPE_BASE ** (-jnp.arange(nf, dtype=F32) / nf)
    ang = jnp.concatenate([row[:, None] * inv, col[:, None] * inv], axis=-1)
    cos, sin = jnp.cos(ang), jnp.sin(ang)
    cos_h = jnp.concatenate([cos, cos], axis=1)
    sin_h = jnp.concatenate([-sin, sin], axis=1)
    cos_t = jnp.concatenate([jnp.tile(cos_h, (1, LANES // AT_D)), jnp.ones((CTX, LANES), F32)], axis=0)
    sin_t = jnp.concatenate([jnp.tile(sin_h, (1, LANES // AT_D)), jnp.zeros((CTX, LANES), F32)], axis=0)
    return cos_t, sin_t


def _expansion():
    w = SSD_R * SSD_P
    ex = np.zeros((2 * SSD_G, 4 * LANES, 2 * w), np.float32)
    for d in range(2):
        for g in range(SSD_G):
            e = np.zeros((LANES, w), np.float32)
            for r in range(SSD_R):
                e[d * SSD_H + g * SSD_R + r, r * SSD_P:(r + 1) * SSD_P] = 1.0
            for piece in range(4):
                ex[d * SSD_G + g, piece * LANES:(piece + 1) * LANES,
                   (piece // 2) * w:(piece // 2 + 1) * w] = e
    return jnp.asarray(ex, BF16)


def _group_mean_matrix():
    w = AT_W + AT_KVW
    idx = np.arange(w) // AT_D
    g = (idx[:, None] == idx[None, :]).astype(np.float32) / AT_D
    return jnp.asarray(np.concatenate([g, g], axis=0), BF16)


def _pad_lanes(v, width=LANES):
    return jnp.pad(v, ((0, 0), (0, width - v.shape[1])))


def kernel(x, c, ctx, c_ctx, w_mod, b_mod, w_in, hy_conv_w, hy_conv_b, hy_ffn_w1, hy_ffn_b1, hy_ffn_w2, hy_ffn_b2, hy_ffn_w3, hy_freq, hy_bias, ssd_conv_w, ssd_conv_b, ssd_dt_bias, ssd_a_log, ssd_d, ssd_norm_w, attn_q_norm, attn_k_norm, w_out, ln1_g, ln1_b, ffn_w_up, ffn_conv_w, ffn_conv_b, ffn_w_down, ln2_g, ln2_b):
    nlat = x.shape[1]
    assert x.shape[0] == 1 and ctx.shape[1] == CTX == SSD_BLK and nlat % (N2 * 8) == 0
    t = nlat + CTX
    nlt = nlat // TM
    alpha = (2.0 * DEPTH) ** 0.25
    cb = 16

    x_main, x_ctx = x[0], ctx[0]
    cvec = jnp.zeros((8, D_MODEL), F32).at[0].set(c[0]).at[1].set(c_ctx)
    mod = _modulation(cvec, w_mod, b_mod)
    cos_t, sin_t = _rope_tables(nlat)
    ex = _expansion()
    gsum = _group_mean_matrix()
    tb = _fft_tables(nlat)

    for i in range(DEPTH):
        ctx_out = i < DEPTH - 1
        mv = mod[i, 0:2].reshape(2, 6, 1, D_MODEL)
        sh1, sc1, g1, sh2, sc2, g2 = (mv[:, q] for q in range(6))

        wi = w_in[i]
        o = HY_COLS
        w_z = wi[:, o:o + SSD_W]
        w_xbc = wi[:, o + SSD_W:o + SSD_W + SSD_XBC]
        w_dt = _pad_lanes(wi[:, o + SSD_W + SSD_XBC:o + SSD_W + SSD_XBC + 2 * SSD_H])
        w_at = wi[:, o + SSD_W + SSD_XBC + 2 * SSD_H:]
        w_main = jnp.concatenate([w_z, w_xbc, w_dt, w_at], axis=1).astype(BF16)
        w_hyT = wi[:, 0:HY_COLS].T.astype(BF16)

        hyT, z, xbc, dt, qkv = _in_proj(x_main, x_ctx, t, sc1, sh1, w_main, w_hyT)

        w1 = hy_ffn_w1[i]
        w1t = _pad_lanes(jnp.concatenate([w1[1:], w1[0:1]], axis=0).T)
        fargs = (w1t, hy_ffn_b1[i][:, None], hy_ffn_w2[i].T, hy_ffn_b2[i][:, None],
                 hy_ffn_w3[i].T, hy_freq[i][:, None])
        taps2d = jnp.concatenate([hy_conv_w[i].T, hy_conv_b[i][:, None]], axis=1)
        taps = taps2d.reshape(-1)
        skip = hy_bias[i].reshape(-1)
        filt = _hyena_filters(nlat, *fargs).reshape(4, HY_W, tb["r1"], N2)
        kf = _filter_spectra(filt, tb, cb)
        hy3 = hyT.reshape(HY_COLS, t // N2, N2)
        zz = _hyena_conv(taps, skip, hy3, 0, hy3, HY_W // cb, kf, tb, cb,
                         conv_u=True, u_off=0, g_off=HY_W, order=0)
        y_hy = _hyena_conv(taps, skip, zz, 0, hy3, 2 * HY_W // cb, kf, tb, cb,
                           conv_u=False, u_off=0, g_off=2 * HY_W, order=1)
        y_hy = y_hy.reshape(HY_W, nlat)
        if ctx_out:
            filt_c = _hyena_filters(CTX, *fargs)
            y_hy_c = _hyena_ctx(hyT, nlat // CTX, filt_c, taps2d, skip[:, None])
        else:
            y_hy_c = None

        d_exp = jnp.repeat(ssd_d[i], SSD_P)[None, :]
        yf, yb = _ssd(xbc, dt, ssd_conv_w[i], ssd_conv_b[i][None, :],
                      _pad_lanes(ssd_dt_bias[i].reshape(1, -1)), _pad_lanes(ssd_a_log[i].reshape(1, -1)),
                      d_exp, ex, nlat // SSD_BLK)

        qz, kk, v0, v1 = _attn_prep(qkv, cos_t, sin_t, jnp.tile(attn_q_norm[i], AT_H)[None, :],
                                    jnp.tile(attn_k_norm[i], AT_KV)[None, :], gsum)
        y_at = _flash_attention(qz, kk, v0, v1, nlat, 512, t)
        if ctx_out:
            y_at_c = _flash_attention(qz[:, :, nlat:], kk[nlat:], v0[nlt:], v1[nlt:], CTX, 128, CTX)
        else:
            y_at_c = None

        x1, h2 = _out_proj(y_hy, y_hy_c, yf, yb, z, y_at, y_at_c, x_main, x_ctx, g1, sc2, sh2,
                           w_out[i].astype(BF16), ssd_norm_w[i][None, :], ln1_g[i][None, :],
                           ln1_b[i][None, :], with_ctx=ctx_out, alpha=alpha)
        ntiles = nlt + 1 if ctx_out else nlt
        x_main = _conv_ffn(h2, x1, g2, ffn_w_up[i].astype(BF16), ffn_conv_w[i], ffn_conv_b[i][None, :],
                           ffn_w_down[i].astype(BF16), ln2_g[i][None, :], ln2_b[i][None, :],
                           nlat_tiles=nlt, ntiles=ntiles, alpha=alpha)
        x_ctx = None
    return x_main[None]
```

```python
import functools
import math

import numpy as np
import jax
import jax.numpy as jnp
from jax import lax
from jax.experimental import pallas as pl
from jax.experimental.pallas import tpu as pltpu

F32 = jnp.float32
BF16 = jnp.bfloat16
HI = lax.Precision.HIGHEST

D_MODEL = 1024
DEPTH = 2
GRID_W = 64
CTX = 256
EPS = 1e-6

HY_W = 256
HY_BANDS = 16
HY_EMB = 1 + 2 * HY_BANDS
HY_HID = 64
HY_MIN_DECAY = math.log(1e-2) / 1.5
HY_MAX_DECAY = math.log(1e-2) / 0.3
HY_COLS = 3 * HY_W

SSD_H = 8
SSD_P = 64
SSD_W = SSD_H * SSD_P
SSD_G = 2
SSD_R = SSD_H // SSD_G
SSD_N = 128
SSD_XBC = SSD_W + 2 * SSD_G * SSD_N
CHUNK = 128
SSD_BLK = 2 * CHUNK

AT_H = 4
AT_KV = 2
AT_D = 64
AT_W = AT_H * AT_D
AT_KVW = AT_KV * AT_D
ROPE_BASE = 10000.0

MIX_W = HY_W + SSD_W + AT_W
D_FF = 2816
FF_CHUNK = 256

TM = CTX
LANES = 128
VT_ROWS = AT_D + 16
N2 = 256
V7X_VMEM_LIMIT = 56 * 1024 * 1024


def _cparams(sem, vmem=None):
    return pltpu.CompilerParams(dimension_semantics=sem, vmem_limit_bytes=vmem)


def _dot(a, b):
    return jnp.dot(a.astype(BF16), b.astype(BF16), preferred_element_type=F32)


def _dot_hi(a, b):
    return jnp.dot(a, b, preferred_element_type=F32, precision=HI)


def _split(x, n):
    pieces = []
    for _ in range(n):
        p = x.astype(BF16)
        pieces.append(p)
        x = x - p.astype(F32)
    return pieces


def _sigmoid(x):
    return 1.0 / (1.0 + jnp.exp(-x))


def _silu(x):
    return x * _sigmoid(x)


def _full(shape):
    nd = len(shape)
    return pl.BlockSpec(shape, lambda *_: (0,) * nd)


def _mod_body(c_ref, w_ref, b_ref, o_ref):
    s = _silu(c_ref[...])
    o_ref[0] = _dot(s, w_ref[0]) + b_ref[0]


def _modulation(cvec, w_mod, b_mod):
    depth, d, n = w_mod.shape
    nb = 1536
    return pl.pallas_call(
        _mod_body,
        grid=(depth, n // nb),
        in_specs=[pl.BlockSpec((8, d), lambda l, j: (0, 0)),
                  pl.BlockSpec((1, d, nb), lambda l, j: (l, 0, j)),
                  pl.BlockSpec((1, 1, nb), lambda l, j: (l, 0, j))],
        out_specs=pl.BlockSpec((1, 8, nb), lambda l, j: (l, 0, j)),
        out_shape=jax.ShapeDtypeStruct((depth, 8, n), F32),
        compiler_params=_cparams(("arbitrary", "arbitrary")),
        name="modulation",
    )(cvec, w_mod, b_mod.reshape(depth, 1, n))


W_MAIN = SSD_W + SSD_XBC + LANES + AT_W + 2 * AT_KVW


def _inproj_body(*refs, split_x, nlat_tiles):
    if split_x:
        x_ref, xc_ref, *refs = refs
        x = jnp.where(pl.program_id(0) == nlat_tiles, xc_ref[...], x_ref[...])
    else:
        x_ref, *refs = refs
        x = x_ref[...]
    sc_ref, sh_ref, wm_ref, wh_ref, hy_ref, z_ref, xbc_ref, dt_ref, qkv_ref = refs
    h = (x * (1.0 + sc_ref[0]) + sh_ref[0]).astype(BF16)
    main = jnp.dot(h, wm_ref[...], preferred_element_type=F32)
    z_ref[...] = main[:, 0:SSD_W]
    xbc_ref[...] = main[:, SSD_W:SSD_W + SSD_XBC]
    o = SSD_W + SSD_XBC
    dt_ref[...] = main[:, o:o + LANES]
    qkv_ref[...] = main[:, o + LANES:]
    hy_ref[...] = lax.dot_general(wh_ref[...], h, (((1,), (1,)), ((), ())),
                                  preferred_element_type=F32)


def _in_proj(x_main, x_ctx, t, sc, sh, w_main, w_hyT):
    nt = t // TM
    sel = lambda i: (jnp.where(i == nt - 1, 1, 0), 0, 0)
    row = lambda i: (i, 0)
    split_x = x_ctx is not None
    x_specs = [pl.BlockSpec((TM, D_MODEL), lambda i: (jnp.minimum(i, nt - 2), 0)),
               _full((TM, D_MODEL))] if split_x else [pl.BlockSpec((TM, D_MODEL), row)]
    x_args = [x_main, x_ctx] if split_x else [x_main]
    return pl.pallas_call(
        functools.partial(_inproj_body, split_x=split_x, nlat_tiles=nt - 1),
        grid=(nt,),
        in_specs=x_specs + [
                  pl.BlockSpec((1, 1, D_MODEL), sel),
                  pl.BlockSpec((1, 1, D_MODEL), sel),
                  _full((D_MODEL, W_MAIN)),
                  _full((HY_COLS, D_MODEL))],
        out_specs=[pl.BlockSpec((HY_COLS, TM), lambda i: (0, i)),
                   pl.BlockSpec((TM, SSD_W), row),
                   pl.BlockSpec((TM, SSD_XBC), row),
                   pl.BlockSpec((TM, LANES), row),
                   pl.BlockSpec((TM, AT_W + 2 * AT_KVW), row)],
        out_shape=[jax.ShapeDtypeStruct((HY_COLS, t), F32),
                   jax.ShapeDtypeStruct((t, SSD_W), F32),
                   jax.ShapeDtypeStruct((t, SSD_XBC), F32),
                   jax.ShapeDtypeStruct((t, LANES), F32),
                   jax.ShapeDtypeStruct((t, AT_W + 2 * AT_KVW), F32)],
        compiler_params=_cparams(("arbitrary",), V7X_VMEM_LIMIT),
        name="in_proj",
    )(*x_args, sc, sh, w_main, w_hyT)


def _filter_body(w1_ref, b1_ref, w2_ref, b2_ref, w3_ref, fr_ref, o_ref, *, n, nt):
    j = pl.program_id(0)
    k = (lax.broadcasted_iota(jnp.int32, (1, nt), 1) + j * nt).astype(F32)
    t = k * (1.0 / (n - 1))
    wk = k * (2.0 * math.pi / n)
    band = lax.broadcasted_iota(jnp.int32, (HY_BANDS, nt), 0).astype(F32)
    arg = (1e-4 + band * ((HY_BANDS - 1 - 1e-4) / (HY_BANDS - 1))) * wk
    r8 = lax.broadcasted_iota(jnp.int32, (8, nt), 0)
    feats = jnp.concatenate([jnp.cos(arg), -jnp.sin(arg), jnp.where(r8 == 0, t, 0.0),
                             jnp.zeros((LANES - 2 * HY_BANDS - 8, nt), F32)], axis=0)
    fr = fr_ref[...]
    h = jnp.sin(fr * (_dot_hi(w1_ref[...], feats) + b1_ref[...]))
    h = jnp.sin(fr * (_dot_hi(w2_ref[...], h) + b2_ref[...]))
    h = _dot(w3_ref[...], h)
    c = lax.broadcasted_iota(jnp.int32, (HY_W, nt), 0).astype(F32)
    delta = jnp.abs(HY_MIN_DECAY + c * ((HY_MAX_DECAY - HY_MIN_DECAY) / (HY_W - 1)))
    window = jnp.exp(-t * delta)
    for q in range(4):
        o_ref[q] = h[q * HY_W:(q + 1) * HY_W] * window


def _hyena_filters(n, w1t, b1, w2t, b2, w3t, freq):
    nt = min(n, 2048)
    return pl.pallas_call(
        functools.partial(_filter_body, n=n, nt=nt),
        grid=(n // nt,),
        in_specs=[_full((HY_HID, LANES)), _full((HY_HID, 1)), _full((HY_HID, HY_HID)),
                  _full((HY_HID, 1)), _full((4 * HY_W, HY_HID)), _full((HY_HID, 1))],
        out_specs=pl.BlockSpec((4, HY_W, nt), lambda j: (0, 0, j)),
        out_shape=jax.ShapeDtypeStruct((4, HY_W, n), F32),
        compiler_params=_cparams(("arbitrary",)),
        name="hyena_filters",
    )(w1t, b1, w2t, b2, w3t, freq)


def _fft_tables(n):
    big = 2 * n
    n1 = big // N2
    r1 = n1 // 2
    kp = -(-(r1 + 1) // 8) * 8
    k1 = np.arange(kp)[:, None].astype(np.float64)
    live = (k1 <= r1)
    a1 = 2 * np.pi * k1 * np.arange(r1)[None, :] / n1
    f1a = np.concatenate([np.cos(a1) * live, -np.sin(a1) * live], axis=0)
    at = 2 * np.pi * k1 * np.arange(N2)[None, :] / big
    twr, twi = np.cos(at) * live, -np.sin(at) * live
    a2 = 2 * np.pi * np.outer(np.arange(N2), np.arange(N2)) / N2
    c2, s2 = np.cos(a2), np.sin(a2)
    w2f = np.block([[c2, -s2], [s2, c2]])
    w2i = np.block([[c2, s2], [-s2, c2]])
    wgt = np.where((k1 == 0) | (k1 == r1), 1.0, 2.0) * live / big
    f1i = np.concatenate([(np.cos(a1) * wgt).T, (-np.sin(a1) * wgt).T], axis=1)
    j = jnp.asarray
    return dict(r1=r1, kp=kp, f1a=j(f1a, BF16), twr=j(twr, F32), twi=j(twi, F32),
                w2f=j(w2f, BF16), w2i=j(w2i, BF16), f1i=j(f1i, BF16))


def _fft_rows(x, f1a_ref, twr_ref, twi_ref, kp):
    a = jnp.dot(f1a_ref[...], x.astype(BF16), preferred_element_type=F32)
    ar, ai = a[:kp], a[kp:]
    twr, twi = twr_ref[...], twi_ref[...]
    return jnp.concatenate([ar * twr - ai * twi, ar * twi + ai * twr], axis=1)


def _kspec_body(hf_ref, hb_ref, f1a_ref, twr_ref, twi_ref, w2f_ref, o_ref, sa_ref, *, cb, kp, r1):
    row = lax.broadcasted_iota(jnp.int32, (r1, N2), 0)
    lane = lax.broadcasted_iota(jnp.int32, (r1, N2), 1)
    first = (row == 0) & (lane == 0)

    for c in range(cb):
        af = _fft_rows(hf_ref[0, c], f1a_ref, twr_ref, twi_ref, kp)
        ab = _fft_rows(jnp.where(first, 0.0, hb_ref[0, c]), f1a_ref, twr_ref, twi_ref, kp)
        sa_ref[c * kp:(c + 1) * kp, :] = af + ab
        sa_ref[(cb + c) * kp:(cb + c + 1) * kp, :] = af - ab
    kr = jnp.dot(sa_ref[0:cb * kp, :].astype(BF16), w2f_ref[:, 0:N2], preferred_element_type=F32)
    ki = jnp.dot(sa_ref[cb * kp:, :].astype(BF16), w2f_ref[:, N2:], preferred_element_type=F32)
    o_ref[0] = jnp.concatenate([kr, ki], axis=1).reshape(cb, kp, 2 * N2)


def _filter_spectra(filt, tb, cb):
    r1, kp = tb["r1"], tb["kp"]
    return pl.pallas_call(
        functools.partial(_kspec_body, cb=cb, kp=kp, r1=r1),
        grid=(2, HY_W // cb),
        in_specs=[pl.BlockSpec((1, cb, r1, N2), lambda o, j: (2 * o, j, 0, 0)),
                  pl.BlockSpec((1, cb, r1, N2), lambda o, j: (2 * o + 1, j, 0, 0)),
                  _full((2 * kp, r1)), _full((kp, N2)), _full((kp, N2)), _full((2 * N2, 2 * N2))],
        out_specs=pl.BlockSpec((1, cb, kp, 2 * N2), lambda o, j: (o, j, 0, 0)),
        out_shape=jax.ShapeDtypeStruct((2, HY_W, kp, 2 * N2), F32),
        scratch_shapes=[pltpu.VMEM((2 * cb * kp, 2 * N2), F32)],
        compiler_params=_cparams(("arbitrary", "arbitrary"), V7X_VMEM_LIMIT),
        name="hyena_filter_spectra",
    )(filt, filt, tb["f1a"], tb["twr"], tb["twi"], tb["w2f"])


def _short_conv(x, w0, w1, w2, b):
    r1 = x.shape[0]
    row = lax.broadcasted_iota(jnp.int32, x.shape, 0)
    lane = lax.broadcasted_iota(jnp.int32, x.shape, 1)
    a = pltpu.roll(x, 1, 1)
    prev = jnp.where(lane == 0, jnp.where(row == 0, 0.0, pltpu.roll(a, 1, 0)), a)
    a = pltpu.roll(x, N2 - 1, 1)
    nxt = jnp.where(lane == N2 - 1, jnp.where(row == r1 - 1, 0.0, pltpu.roll(a, r1 - 1, 0)), a)
    return w0 * prev + w1 * x + w2 * nxt + b


def _hyconv_body(taps_ref, skip_ref, u_ref, g_ref, kf_ref, f1a_ref, twr_ref, twi_ref, w2f_ref,
                 w2i_ref, f1i_ref, o_ref, sa_ref, sq_ref, sx_ref, *, cb, kp, conv_u, u_off,
                 g_off, order):
    j = pl.program_id(0)

    def taps(ch):
        return (taps_ref[ch * 4], taps_ref[ch * 4 + 1], taps_ref[ch * 4 + 2], taps_ref[ch * 4 + 3])

    for c in range(cb):
        x = u_ref[c]
        if conv_u:
            x = _short_conv(x, *taps(u_off + j * cb + c))
        sx_ref[c] = x
        sa_ref[c * kp:(c + 1) * kp, :] = _fft_rows(x, f1a_ref, twr_ref, twi_ref, kp)
    x = jnp.dot(sa_ref[...].astype(BF16), w2f_ref[...], preferred_element_type=F32)
    k = kf_ref[0].reshape(cb * kp, 2 * N2)
    xr, xi, kr, ki = x[:, :N2], x[:, N2:], k[:, :N2], k[:, N2:]
    p = jnp.concatenate([xr * kr - xi * ki, xr * ki + xi * kr], axis=1).astype(BF16)
    sq_ref[...] = jnp.dot(p, w2i_ref[...], preferred_element_type=F32)

    for c in range(cb):
        q = sq_ref[c * kp:(c + 1) * kp, :]
        qr, qi = q[:, :N2], q[:, N2:]
        twr, twi = twr_ref[...], twi_ref[...]
        y2 = jnp.concatenate([qr * twr + qi * twi, qi * twr - qr * twi], axis=0).astype(BF16)
        y = jnp.dot(f1i_ref[...], y2, preferred_element_type=F32)
        y = y + skip_ref[order * HY_W + j * cb + c] * sx_ref[c]
        o_ref[c] = _short_conv(g_ref[c], *taps(g_off + j * cb + c)) * y


def _hyena_conv(taps, skip, u3, u_blk0, g3, g_blk0, kf, tb, cb, *, conv_u, u_off, g_off, order):
    r1, kp = tb["r1"], tb["kp"]
    smem = pl.BlockSpec(memory_space=pltpu.SMEM)
    return pl.pallas_call(
        functools.partial(_hyconv_body, cb=cb, kp=kp, conv_u=conv_u, u_off=u_off, g_off=g_off,
                          order=order),
        grid=(HY_W // cb,),
        in_specs=[smem, smem,
                  pl.BlockSpec((cb, r1, N2), lambda j: (j + u_blk0, 0, 0)),
                  pl.BlockSpec((cb, r1, N2), lambda j: (j + g_blk0, 0, 0)),
                  pl.BlockSpec((1, cb, kp, 2 * N2), lambda j: (order, j, 0, 0)),
                  _full((2 * kp, r1)), _full((kp, N2)), _full((kp, N2)),
                  _full((2 * N2, 2 * N2)), _full((2 * N2, 2 * N2)), _full((r1, 2 * kp))],
        out_specs=pl.BlockSpec((cb, r1, N2), lambda j: (j, 0, 0)),
        out_shape=jax.ShapeDtypeStruct((HY_W, r1, N2), F32),
        scratch_shapes=[pltpu.VMEM((cb * kp, 2 * N2), F32),
                        pltpu.VMEM((cb * kp, 2 * N2), F32),
                        pltpu.VMEM((cb, r1, N2), F32)],
        compiler_params=_cparams(("arbitrary",), V7X_VMEM_LIMIT),
        name="hyena_conv%d" % order,
    )(taps, skip, u3, g3, kf, tb["f1a"], tb["twr"], tb["twi"], tb["w2f"], tb["w2i"], tb["f1i"])


def _ctx_tables(n):
    big = 2 * n
    a = 2 * np.pi * np.outer(np.arange(n), np.arange(big)) / big
    fc = np.concatenate([np.cos(a), -np.sin(a)], axis=1)
    gi = np.concatenate([np.cos(a).T, -np.sin(a).T], axis=0) / big
    return jnp.asarray(fc, BF16), jnp.asarray(gi, BF16)


def _hyctx_body(hy_ref, filt_ref, taps_ref, skip_ref, fc_ref, gi_ref, o_ref, *, n):
    big = 2 * n
    lane = lax.broadcasted_iota(jnp.int32, (HY_W, n), 1)

    def sconv(x, t):
        prev = jnp.where(lane == 0, 0.0, pltpu.roll(x, 1, 1))
        nxt = jnp.where(lane == n - 1, 0.0, pltpu.roll(x, n - 1, 1))
        return t[:, 0:1] * prev + t[:, 1:2] * x + t[:, 2:3] * nxt + t[:, 3:4]

    def spec(o):
        xf = _dot(filt_ref[2 * o], fc_ref[...])
        xb = _dot(jnp.where(lane == 0, 0.0, filt_ref[2 * o + 1]), fc_ref[...])
        return xf[:, :big] + xb[:, :big], xf[:, big:] - xb[:, big:]

    def lconv(u, o):
        kr, ki = spec(o)
        x = _dot(u, fc_ref[...])
        xr, xi = x[:, :big], x[:, big:]
        p = jnp.concatenate([xr * kr - xi * ki, xr * ki + xi * kr], axis=1)
        return _dot(p, gi_ref[...]) + skip_ref[o * HY_W:(o + 1) * HY_W, :] * u

    v = sconv(hy_ref[0:HY_W, :], taps_ref[0:HY_W, :])
    x1 = sconv(hy_ref[HY_W:2 * HY_W, :], taps_ref[HY_W:2 * HY_W, :])
    x2 = sconv(hy_ref[2 * HY_W:, :], taps_ref[2 * HY_W:, :])
    z = x1 * lconv(v, 0)
    o_ref[...] = x2 * lconv(z, 1)


def _hyena_ctx(hyT, blk, filt_ctx, taps2d, skip2d):
    n = filt_ctx.shape[-1]
    fc, gi = _ctx_tables(n)
    return pl.pallas_call(
        functools.partial(_hyctx_body, n=n),
        grid=(1,),
        in_specs=[pl.BlockSpec((HY_COLS, n), lambda i: (0, blk)),
                  _full((4, HY_W, n)), _full((HY_COLS, 4)), _full((2 * HY_W, 1)),
                  _full((n, 4 * n)), _full((4 * n, n))],
        out_specs=_full((HY_W, n)),
        out_shape=jax.ShapeDtypeStruct((HY_W, n), F32),
        compiler_params=_cparams(("arbitrary",), V7X_VMEM_LIMIT),
        name="hyena_ctx",
    )(hyT, filt_ctx, taps2d, skip2d, fc, gi)


def _ssd_body(xf_ref, xfp_ref, xfn_ref, dtf_ref, xb_ref, xbp_ref, xbn_ref, dtb_ref,
              cw_ref, cb_ref, dtbias_ref, alog_ref, dexp_ref, ex_ref,
              yf_ref, yb_ref, h_ref, sx_ref, *, nlat):
    j = pl.program_id(0)
    nblk = nlat + 1

    @pl.when(j == 0)
    def _():
        h_ref[...] = jnp.zeros_like(h_ref)

    ri = lax.broadcasted_iota(jnp.int32, (CHUNK, CHUNK), 0)
    ci = lax.broadcasted_iota(jnp.int32, (CHUNK, CHUNK), 1)
    a_all = -jnp.exp(alog_ref[...])

    for d in range(2):
        x_ref, xp_ref, xn_ref, dt_ref, y_ref = (
            (xf_ref, xfp_ref, xfn_ref, dtf_ref, yf_ref) if d == 0
            else (xb_ref, xbp_ref, xbn_ref, dtb_ref, yb_ref))
        bid = jnp.where(j < 1, nlat, j - 1) if d == 0 else nblk - 1 - j
        pv = ((bid != 0) & (bid != nlat)).astype(F32)
        nv = ((bid != nlat - 1) & (bid != nblk - 1)).astype(F32)
        sx_ref[0:8, :] = xp_ref[...] * pv
        sx_ref[8:8 + SSD_BLK, :] = x_ref[...]
        sx_ref[8 + SSD_BLK:, :] = xn_ref[...] * nv
        pre = (cw_ref[0:1, :] * sx_ref[pl.ds(7, SSD_BLK), :] + cw_ref[1:2, :] * x_ref[...]
               + cw_ref[2:3, :] * sx_ref[pl.ds(9, SSD_BLK), :] + cb_ref[...])
        xc_blk = _silu(pre)
        z = dt_ref[...] + dtbias_ref[...]
        dtv_blk = jnp.maximum(z, 0.0) + jnp.log(1.0 + jnp.exp(-jnp.abs(z)))
        y_parts = [None, None]
        for sub in ((0, 1) if d == 0 else (1, 0)):
            y_parts[sub] = _ssd_chunk(d, xc_blk[sub * CHUNK:(sub + 1) * CHUNK],
                                      dtv_blk[sub * CHUNK:(sub + 1) * CHUNK], a_all, ri, ci,
                                      dexp_ref, ex_ref, h_ref)
        y_ref[...] = jnp.concatenate(y_parts, axis=0)


def _ssd_chunk(d, xc, dtv, a_all, ri, ci, dexp_ref, ex_ref, h_ref):
    xs = xc[:, 0:SSD_W]
    da = dtv * a_all
    tri = (ri >= ci) if d == 0 else (ri <= ci)
    r3 = jnp.dot(tri.astype(BF16), jnp.concatenate(_split(da, 3), axis=1),
                 preferred_element_type=F32)
    acum = r3[:, 0:LANES] + r3[:, LANES:2 * LANES] + r3[:, 2 * LANES:]
    last = CHUNK - 1 if d == 0 else 0
    acum_t = acum.T
    pieces = jnp.concatenate(_split(acum, 2) + _split(dtv, 2), axis=1)
    ys = []
    for g in range(SSD_G):
        bm = xc[:, SSD_W + g * SSD_N:SSD_W + (g + 1) * SSD_N]
        cm = xc[:, SSD_W + (SSD_G + g) * SSD_N:SSD_W + (SSD_G + g + 1) * SSD_N]
        bt = bm.T
        scores = _dot(cm, bt)
        both = jnp.dot(pieces, ex_ref[d * SSD_G + g], preferred_element_type=F32)
        acum_e, dt_e = both[:, 0:SSD_R * SSD_P], both[:, SSD_R * SSD_P:]
        tot_e = acum_e[last:last + 1, :]
        xg = xs[:, g * SSD_R * SSD_P:(g + 1) * SSD_R * SSD_P]
        hg = h_ref[d, g]
        yoff = _dot(cm, hg) * jnp.exp(acum_e)
        ydiag = []
        for r in range(SSD_R):
            hl = d * SSD_H + g * SSD_R + r
            seg = acum[:, hl:hl + 1] - acum_t[hl:hl + 1, :]
            m = scores * jnp.exp(jnp.where(tri, seg, -jnp.inf))
            xdt = xg[:, r * SSD_P:(r + 1) * SSD_P] * dt_e[:, r * SSD_P:(r + 1) * SSD_P]
            ydiag.append(_dot(m, xdt))
        ys.append(jnp.concatenate(ydiag, axis=1) + yoff)
        xw = xg * (jnp.exp(tot_e - acum_e) * dt_e)
        h_ref[d, g] = hg * jnp.exp(tot_e) + _dot(bt, xw)
    y = jnp.concatenate(ys, axis=1)
    if d == 0:
        y = y + dexp_ref[...] * xs
    return y


def _ssd(xbc, dt, conv_w, conv_b, dt_bias, a_log, d_exp, ex, nlat):
    t = xbc.shape[0]
    nblk = t // SSD_BLK
    hb = SSD_BLK // 8
    last8 = t // 8 - 1
    fc = lambda j: jnp.where(j < 1, nlat, j - 1)
    bc = lambda j: nblk - 1 - j
    prev = lambda f: (lambda j: (jnp.maximum(f(j) * hb - 1, 0), 0))
    nxt = lambda f: (lambda j: (jnp.minimum((f(j) + 1) * hb, last8), 0))
    main = lambda f: (lambda j: (f(j), 0))
    return pl.pallas_call(
        functools.partial(_ssd_body, nlat=nlat),
        grid=(nblk,),
        in_specs=[pl.BlockSpec((SSD_BLK, SSD_XBC), main(fc)), pl.BlockSpec((8, SSD_XBC), prev(fc)),
                  pl.BlockSpec((8, SSD_XBC), nxt(fc)), pl.BlockSpec((SSD_BLK, LANES), main(fc)),
                  pl.BlockSpec((SSD_BLK, SSD_XBC), main(bc)), pl.BlockSpec((8, SSD_XBC), prev(bc)),
                  pl.BlockSpec((8, SSD_XBC), nxt(bc)), pl.BlockSpec((SSD_BLK, LANES), main(bc)),
                  _full((3, SSD_XBC)), _full((1, SSD_XBC)), _full((1, LANES)), _full((1, LANES)),
                  _full((1, SSD_W)), _full((2 * SSD_G, 4 * LANES, 2 * SSD_R * SSD_P))],
        out_specs=[pl.BlockSpec((SSD_BLK, SSD_W), main(fc)), pl.BlockSpec((SSD_BLK, SSD_W), main(bc))],
        out_shape=[jax.ShapeDtypeStruct((t, SSD_W), F32), jax.ShapeDtypeStruct((t, SSD_W), F32)],
        scratch_shapes=[pltpu.VMEM((2, SSD_G, SSD_N, SSD_R * SSD_P), F32),
                        pltpu.VMEM((SSD_BLK + 16, SSD_XBC), F32)],
        compiler_params=_cparams(("arbitrary",), V7X_VMEM_LIMIT),
        name="ssd_scan",
    )(xbc, xbc, xbc, dt, xbc, xbc, xbc, dt, conv_w, conv_b, dt_bias, a_log, d_exp, ex)


def _attn_prep_body(qkv_ref, cos_ref, sin_ref, qn_ref, kn_ref, gsum_ref, q_ref, k_ref, v0_ref, v1_ref):
    x = qkv_ref[...]
    qk = x[:, 0:AT_W + AT_KVW]
    ms = jnp.dot(jnp.concatenate(_split(qk * qk, 2), axis=1), gsum_ref[...],
                 preferred_element_type=F32)
    wn = jnp.concatenate([qn_ref[...], kn_ref[...]], axis=1)
    qk = qk * lax.rsqrt(ms + EPS) * wn
    w = qk.shape[1]
    lane = lax.broadcasted_iota(jnp.int32, qk.shape, 1)
    partner = jnp.where(lane % AT_D < AT_D // 2, pltpu.roll(qk, w - AT_D // 2, 1),
                        pltpu.roll(qk, AT_D // 2, 1))
    cos = jnp.concatenate([cos_ref[...]] * 3, axis=1)
    sin = jnp.concatenate([sin_ref[...]] * 3, axis=1)
    qk = qk * cos + partner * sin
    q = qk[:, 0:AT_W] * (AT_D ** -0.5 * math.log2(math.e))
    zeros = jnp.zeros((q.shape[0], AT_D), F32)
    for h in range(AT_H):
        qh = q[:, h * AT_D:(h + 1) * AT_D]
        qh = (jnp.concatenate([qh, zeros], axis=1) if h // 2 == 0
              else jnp.concatenate([zeros, qh], axis=1))
        q_ref[h] = qh.T.astype(BF16)
    k_ref[...] = qk[:, AT_W:].astype(BF16)
    v = x[:, AT_W + AT_KVW:]
    vl = lax.broadcasted_iota(jnp.int32, v.shape, 1)
    for g, v_ref in enumerate((v0_ref, v1_ref)):
        vg = v if g == 0 else pltpu.roll(v, AT_D, 1)
        vg = jnp.where(vl < AT_D, vg, jnp.where(vl == AT_D, 1.0, 0.0))
        v_ref[0] = vg.T[0:VT_ROWS].astype(BF16)


def _attn_prep(qkv, cos_t, sin_t, qn, kn, gsum):
    t = qkv.shape[0]
    row = lambda i: (i, 0)
    return pl.pallas_call(
        _attn_prep_body,
        grid=(t // TM,),
        in_specs=[pl.BlockSpec((TM, AT_W + 2 * AT_KVW), row), pl.BlockSpec((TM, LANES), row),
                  pl.BlockSpec((TM, LANES), row), _full((1, AT_W)), _full((1, AT_KVW)),
                  _full((2 * (AT_W + AT_KVW), AT_W + AT_KVW))],
        out_specs=[pl.BlockSpec((AT_H, LANES, TM), lambda i: (0, 0, i)),
                   pl.BlockSpec((TM, LANES), row),
                   pl.BlockSpec((1, VT_ROWS, TM), lambda i: (i, 0, 0)),
                   pl.BlockSpec((1, VT_ROWS, TM), lambda i: (i, 0, 0))],
        out_shape=[jax.ShapeDtypeStruct((AT_H, LANES, t), BF16),
                   jax.ShapeDtypeStruct((t, LANES), BF16),
                   jax.ShapeDtypeStruct((t // TM, VT_ROWS, TM), BF16),
                   jax.ShapeDtypeStruct((t // TM, VT_ROWS, TM), BF16)],
        compiler_params=_cparams(("arbitrary",)),
        name="attn_prep",
    )(qkv, cos_t, sin_t, qn, kn, gsum)


def _flash_body(q_ref, k_ref, v0_ref, v1_ref, o_ref, s_ref, *, tq, kc, nkc):
    v_refs = (v0_ref, v1_ref)
    nb = kc // TM
    qs = [jnp.concatenate([q_ref[2 * g], q_ref[2 * g + 1]], axis=1) for g in range(AT_KV)]

    def scores(c, slot):
        k = k_ref[pl.ds(pl.multiple_of(c * kc, kc), kc), :]
        for g in range(AT_KV):
            s_ref[slot, g] = jnp.dot(k, qs[g], preferred_element_type=F32)

    def update(c, slot, carry):
        out = []
        for g, (m, acc) in enumerate(carry):
            s = s_ref[slot, g]
            m_new = jnp.maximum(m, jnp.max(s, axis=0, keepdims=True))
            p = jnp.exp2(s - m_new).astype(BF16)
            pv = jnp.dot(v_refs[g][c * nb], p[0:TM], preferred_element_type=F32)
            for b in range(1, nb):
                pv = pv + jnp.dot(v_refs[g][c * nb + b], p[b * TM:(b + 1) * TM],
                                  preferred_element_type=F32)
            out.append((m_new, jnp.exp2(m - m_new) * acc + pv))
        return tuple(out)

    def pair(i, carry):
        scores(2 * i + 1, 1)
        carry = update(2 * i, 0, carry)
        scores(2 * i + 2, 0)
        return update(2 * i + 1, 1, carry)

    scores(0, 0)
    carry = tuple((jnp.full((1, 2 * tq), -jnp.inf, F32), jnp.zeros((VT_ROWS, 2 * tq), F32))
                  for _ in range(AT_KV))
    carry = lax.fori_loop(0, (nkc - 1) // 2, pair, carry)
    if (nkc - 1) % 2:
        scores(nkc - 1, 1)
        carry = update(nkc - 2, 0, carry)
        carry = update(nkc - 1, 1, carry)
    else:
        carry = update(nkc - 1, 0, carry)
    lane = lax.broadcasted_iota(jnp.int32, (tq, LANES), 1)
    outs = []
    for _, acc in carry:
        o = acc[0:AT_D] / acc[AT_D:AT_D + 1, :]
        o = jnp.concatenate([o, jnp.zeros((LANES - AT_D, 2 * tq), F32)], axis=0).T
        outs.append(jnp.where(lane < AT_D, o[:tq], pltpu.roll(o[tq:], AT_D, 1)))
    o_ref[...] = jnp.concatenate(outs, axis=1)


def _flash_attention(qt, k, v0t, v1t, nq_rows, tq, kv_rows, q_blk0=0):
    nd = kv_rows // TM
    kc = TM * max(d for d in range(1, 6) if nd % d == 0)
    kspec = pl.BlockSpec((kv_rows, LANES), lambda i: (0, 0))
    vspec = pl.BlockSpec((nd, VT_ROWS, TM), lambda i: (0, 0, 0))
    return pl.pallas_call(
        functools.partial(_flash_body, tq=tq, kc=kc, nkc=kv_rows // kc),
        grid=(nq_rows // tq,),
        in_specs=[pl.BlockSpec((AT_H, LANES, tq), lambda i: (0, 0, i + q_blk0)),
                  kspec, vspec, vspec],
        out_specs=pl.BlockSpec((tq, AT_W), lambda i: (i, 0)),
        out_shape=jax.ShapeDtypeStruct((nq_rows, AT_W), F32),
        scratch_shapes=[pltpu.VMEM((2, AT_KV, kc, 2 * tq), F32)],
        compiler_params=_cparams(("arbitrary",), V7X_VMEM_LIMIT),
        name="flash_attention",
    )(qt, k, v0t, v1t)


def _layernorm(x, g, b):
    mu = jnp.mean(x, axis=-1, keepdims=True)
    xc = x - mu
    var = jnp.mean(xc * xc, axis=-1, keepdims=True)
    return xc * lax.rsqrt(var + EPS) * g + b


def _outproj_body(*refs, with_ctx, split_x, nlat_tiles, alpha):
    refs = list(refs)
    take = lambda n: [refs.pop(0) for _ in range(n)]
    hyl_ref, = take(1)
    hyc_ref, = take(1) if with_ctx else (None,)
    yf_ref, yb_ref, z_ref, atl_ref = take(4)
    atc_ref, = take(1) if with_ctx else (None,)
    x_ref, = take(1)
    xc_ref, = take(1) if split_x else (None,)
    g1_ref, sc2_ref, sh2_ref, w_ref, nw_ref, lg_ref, lb_ref, x1_ref, h2_ref = refs
    hy, at, x = hyl_ref[...], atl_ref[...], x_ref[...]
    if with_ctx:
        is_ctx = pl.program_id(0) == nlat_tiles
        hy = jnp.where(is_ctx, hyc_ref[...], hy)
        at = jnp.where(is_ctx, atc_ref[...], at)
        if split_x:
            x = jnp.where(is_ctx, xc_ref[...], x)
    m = _dot(hy.T, w_ref[0:HY_W, :])
    gated = (yf_ref[...] + yb_ref[...]) * _silu(z_ref[...])
    gw = SSD_W // SSD_G
    parts = []
    for g in range(SSD_G):
        gg = gated[:, g * gw:(g + 1) * gw]
        parts.append(gg * lax.rsqrt(jnp.mean(gg * gg, axis=-1, keepdims=True) + EPS))
    ssd = jnp.concatenate(parts, axis=1) * nw_ref[...]
    m = m + _dot(ssd, w_ref[HY_W:HY_W + SSD_W, :]) + _dot(at, w_ref[HY_W + SSD_W:, :])
    x1 = _layernorm(alpha * x + g1_ref[0] * m, lg_ref[...], lb_ref[...])
    x1_ref[...] = x1
    h2_ref[...] = (x1 * (1.0 + sc2_ref[0]) + sh2_ref[0]).astype(BF16)


def _out_proj(hy_lat, hy_ctx, yf, yb, z, at_lat, at_ctx, x_main, x_ctx, g1, sc2, sh2, w_out, nw, lg, lb,
              *, with_ctx, alpha):
    t = yf.shape[0]
    split_x = with_ctx and x_ctx is not None
    nlat_tiles = hy_lat.shape[1] // TM
    nt = t // TM if with_ctx else nlat_tiles
    last = t // TM - 1
    sel = lambda i: (jnp.where(i == last, 1, 0), 0, 0)
    row = lambda i: (i, 0)
    lat_row = lambda i: (jnp.minimum(i, nlat_tiles - 1), 0)
    lat_col = lambda i: (0, jnp.minimum(i, nlat_tiles - 1))
    vec = pl.BlockSpec((1, 1, D_MODEL), sel)
    specs = [pl.BlockSpec((HY_W, TM), lat_col)]
    args = [hy_lat]
    if with_ctx:
        specs.append(_full((HY_W, TM)))
        args.append(hy_ctx)
    specs += [pl.BlockSpec((TM, SSD_W), row)] * 3 + [pl.BlockSpec((TM, AT_W), lat_row)]
    args += [yf, yb, z, at_lat]
    if with_ctx:
        specs.append(_full((TM, AT_W)))
        args.append(at_ctx)
    specs.append(pl.BlockSpec((TM, D_MODEL), lat_row if x_ctx is not None else row))
    args.append(x_main)
    if split_x:
        specs.append(_full((TM, D_MODEL)))
        args.append(x_ctx)
    specs += [vec, vec, vec, _full((MIX_W, D_MODEL)),
              _full((1, SSD_W)), _full((1, D_MODEL)), _full((1, D_MODEL))]
    args += [g1, sc2, sh2, w_out, nw, lg, lb]
    return pl.pallas_call(
        functools.partial(_outproj_body, with_ctx=with_ctx, split_x=split_x, nlat_tiles=nlat_tiles,
                          alpha=alpha),
        grid=(nt,),
        in_specs=specs,
        out_specs=[pl.BlockSpec((TM, D_MODEL), row), pl.BlockSpec((TM, D_MODEL), row)],
        out_shape=[jax.ShapeDtypeStruct((nt * TM, D_MODEL), F32),
                   jax.ShapeDtypeStruct((nt * TM, D_MODEL), BF16)],
        compiler_params=_cparams(("arbitrary",), V7X_VMEM_LIMIT),
        name="out_proj",
    )(*args)


HALO = 16


def _ffn_body(h_ref, hp_ref, hn_ref, x1_ref, g2_ref, wup_ref, cw_ref, cb_ref, wdn_ref, lg_ref, lb_ref,
              o_ref, *, nlat_tiles, ntiles, alpha):
    i = pl.program_id(0)
    pv = (i != 0) & (i != nlat_tiles)
    nv = (i != nlat_tiles - 1) & (i != ntiles - 1)
    zero = jnp.zeros((HALO, D_MODEL), BF16)
    hp = jnp.where(pv, hp_ref[...], zero)
    hn = jnp.where(nv, hn_ref[...], zero)
    hext = jnp.concatenate([hp, h_ref[...], hn], axis=0)
    acc = jnp.zeros((TM, D_MODEL), F32)
    wide = 2 * FF_CHUNK
    widths = [wide] * (D_FF // wide) + ([D_FF % wide] if D_FF % wide else [])
    starts = [sum(widths[:c]) for c in range(len(widths))]
    nchunks = len(widths)

    def up(c):
        return [jnp.dot(hext, wup_ref[:, o:o + widths[c]], preferred_element_type=F32)
                for o in (starts[c], D_FF + starts[c])]

    def conv(u, o, w):
        cw = cw_ref[:, o:o + w]
        prev = pltpu.roll(u, 1, 0)[HALO:HALO + TM]
        nxt = pltpu.roll(u, TM + 2 * HALO - 1, 0)[HALO:HALO + TM]
        return (cw[0:1] * prev + cw[1:2] * u[HALO:HALO + TM] + cw[2:3] * nxt
                + cb_ref[:, o:o + w])

    ahead = 2
    queue = [up(c) for c in range(ahead)]
    for c in range(nchunks):
        if c + ahead < nchunks:
            queue.append(up(c + ahead))
        us = queue.pop(0)
        c0, w = starts[c], widths[c]
        act = (_silu(conv(us[0], c0, w)) * conv(us[1], D_FF + c0, w)).astype(BF16)
        acc = acc + jnp.dot(act, wdn_ref[c0:c0 + w, :], preferred_element_type=F32)
    o_ref[...] = _layernorm(alpha * x1_ref[...] + g2_ref[0] * acc, lg_ref[...], lb_ref[...])


def _conv_ffn(h2, x1, g2, w_up, conv_w, conv_b, w_down, lg, lb, *, nlat_tiles, ntiles, alpha):
    hb = TM // HALO
    lastb = ntiles * hb - 1
    row = lambda i: (i, 0)
    sel = lambda i: (jnp.where(i == nlat_tiles, 1, 0), 0, 0)
    return pl.pallas_call(
        functools.partial(_ffn_body, nlat_tiles=nlat_tiles, ntiles=ntiles, alpha=alpha),
        grid=(ntiles,),
        in_specs=[pl.BlockSpec((TM, D_MODEL), row),
                  pl.BlockSpec((HALO, D_MODEL), lambda i: (jnp.maximum(i * hb - 1, 0), 0)),
                  pl.BlockSpec((HALO, D_MODEL), lambda i: (jnp.minimum((i + 1) * hb, lastb), 0)),
                  pl.BlockSpec((TM, D_MODEL), row),
                  pl.BlockSpec((1, 1, D_MODEL), sel),
                  _full((D_MODEL, 2 * D_FF)), _full((3, 2 * D_FF)), _full((1, 2 * D_FF)),
                  _full((D_FF, D_MODEL)), _full((1, D_MODEL)), _full((1, D_MODEL))],
        out_specs=pl.BlockSpec((TM, D_MODEL), row),
        out_shape=jax.ShapeDtypeStruct((ntiles * TM, D_MODEL), F32),
        compiler_params=_cparams(("arbitrary",), V7X_VMEM_LIMIT),
        name="conv_ffn",
    )(h2, h2, h2, x1, g2, w_up, conv_w, conv_b, w_down, lg, lb)


def _rope_tables(nlat):
    rows = nlat // GRID_W
    row = jnp.repeat(jnp.arange(rows, dtype=F32), GRID_W)
    col = jnp.tile(jnp.arange(GRID_W, dtype=F32), rows)
    nf = AT_D // 4
    inv = ROPE_BASE ** (-jnp.arange(nf, dtype=F32) / nf)
    ang = jnp.concatenate([row[:, None] * inv, col[:, None] * inv], axis=-1)
    cos, sin = jnp.cos(ang), jnp.sin(ang)
    cos_h = jnp.concatenate([cos, cos], axis=1)
    sin_h = jnp.concatenate([-sin, sin], axis=1)
    cos_t = jnp.concatenate([jnp.tile(cos_h, (1, LANES // AT_D)), jnp.ones((CTX, LANES), F32)], axis=0)
    sin_t = jnp.concatenate([jnp.tile(sin_h, (1, LANES // AT_D)), jnp.zeros((CTX, LANES), F32)], axis=0)
    return cos_t, sin_t


def _expansion():
    w = SSD_R * SSD_P
    ex = np.zeros((2 * SSD_G, 4 * LANES, 2 * w), np.float32)
    for d in range(2):
        for g in range(SSD_G):
            e = np.zeros((LANES, w), np.float32)
            for r in range(SSD_R):
                e[d * SSD_H + g * SSD_R + r, r * SSD_P:(r + 1) * SSD_P] = 1.0
            for piece in range(4):
                ex[d * SSD_G + g, piece * LANES:(piece + 1) * LANES,
                   (piece // 2) * w:(piece // 2 + 1) * w] = e
    return jnp.asarray(ex, BF16)


def _group_mean_matrix():
    w = AT_W + AT_KVW
    idx = np.arange(w) // AT_D
    g = (idx[:, None] == idx[None, :]).astype(np.float32) / AT_D
    return jnp.asarray(np.concatenate([g, g], axis=0), BF16)


def _pad_lanes(v, width=LANES):
    return jnp.pad(v, ((0, 0), (0, width - v.shape[1])))


def kernel(x, c, ctx, c_ctx, w_mod, b_mod, w_in, hy_conv_w, hy_conv_b, hy_ffn_w1, hy_ffn_b1, hy_ffn_w2, hy_ffn_b2, hy_ffn_w3, hy_freq, hy_bias, ssd_conv_w, ssd_conv_b, ssd_dt_bias, ssd_a_log, ssd_d, ssd_norm_w, attn_q_norm, attn_k_norm, w_out, ln1_g, ln1_b, ffn_w_up, ffn_conv_w, ffn_conv_b, ffn_w_down, ln2_g, ln2_b):
    nlat = x.shape[1]
    assert x.shape[0] == 1 and ctx.shape[1] == CTX == SSD_BLK and nlat % (N2 * 8) == 0
    t = nlat + CTX
    nlt = nlat // TM
    alpha = (2.0 * DEPTH) ** 0.25
    cb = 16

    x_main, x_ctx = x[0], ctx[0]
    cvec = jnp.zeros((8, D_MODEL), F32).at[0].set(c[0]).at[1].set(c_ctx)
    mod = _modulation(cvec, w_mod, b_mod)
    cos_t, sin_t = _rope_tables(nlat)
    ex = _expansion()
    gsum = _group_mean_matrix()
    tb = _fft_tables(nlat)

    for i in range(DEPTH):
        ctx_out = i < DEPTH - 1
        mv = mod[i, 0:2].reshape(2, 6, 1, D_MODEL)
        sh1, sc1, g1, sh2, sc2, g2 = (mv[:, q] for q in range(6))

        wi = w_in[i]
        o = HY_COLS
        w_z = wi[:, o:o + SSD_W]
        w_xbc = wi[:, o + SSD_W:o + SSD_W + SSD_XBC]
        w_dt = _pad_lanes(wi[:, o + SSD_W + SSD_XBC:o + SSD_W + SSD_XBC + 2 * SSD_H])
        w_at = wi[:, o + SSD_W + SSD_XBC + 2 * SSD_H:]
        w_main = jnp.concatenate([w_z, w_xbc, w_dt, w_at], axis=1).astype(BF16)
        w_hyT = wi[:, 0:HY_COLS].T.astype(BF16)

        hyT, z, xbc, dt, qkv = _in_proj(x_main, x_ctx, t, sc1, sh1, w_main, w_hyT)

        w1 = hy_ffn_w1[i]
        w1t = _pad_lanes(jnp.concatenate([w1[1:], w1[0:1]], axis=0).T)
        fargs = (w1t, hy_ffn_b1[i][:, None], hy_ffn_w2[i].T, hy_ffn_b2[i][:, None],
                 hy_ffn_w3[i].T, hy_freq[i][:, None])
        taps2d = jnp.concatenate([hy_conv_w[i].T, hy_conv_b[i][:, None]], axis=1)
        taps = taps2d.reshape(-1)
        skip = hy_bias[i].reshape(-1)
        filt = _hyena_filters(nlat, *fargs).reshape(4, HY_W, tb["r1"], N2)
        kf = _filter_spectra(filt, tb, cb)
        hy3 = hyT.reshape(HY_COLS, t // N2, N2)
        zz = _hyena_conv(taps, skip, hy3, 0, hy3, HY_W // cb, kf, tb, cb,
                         conv_u=True, u_off=0, g_off=HY_W, order=0)
        y_hy = _hyena_conv(taps, skip, zz, 0, hy3, 2 * HY_W // cb, kf, tb, cb,
                           conv_u=False, u_off=0, g_off=2 * HY_W, order=1)
        y_hy = y_hy.reshape(HY_W, nlat)
        if ctx_out:
            filt_c = _hyena_filters(CTX, *fargs)
            y_hy_c = _hyena_ctx(hyT, nlat // CTX, filt_c, taps2d, skip[:, None])
        else:
            y_hy_c = None

        d_exp = jnp.repeat(ssd_d[i], SSD_P)[None, :]
        yf, yb = _ssd(xbc, dt, ssd_conv_w[i], ssd_conv_b[i][None, :],
                      _pad_lanes(ssd_dt_bias[i].reshape(1, -1)), _pad_lanes(ssd_a_log[i].reshape(1, -1)),
                      d_exp, ex, nlat // SSD_BLK)

        qz, kk, v0, v1 = _attn_prep(qkv, cos_t, sin_t, jnp.tile(attn_q_norm[i], AT_H)[None, :],
                                    jnp.tile(attn_k_norm[i], AT_KV)[None, :], gsum)
        y_at = _flash_attention(qz, kk, v0, v1, nlat, 512, t)
        if ctx_out:
            y_at_c = _flash_attention(qz[:, :, nlat:], kk[nlat:], v0[nlt:], v1[nlt:], CTX, 128, CTX)
        else:
            y_at_c = None

        x1, h2 = _out_proj(y_hy, y_hy_c, yf, yb, z, y_at, y_at_c, x_main, x_ctx, g1, sc2, sh2,
                           w_out[i].astype(BF16), ssd_norm_w[i][None, :], ln1_g[i][None, :],
                           ln1_b[i][None, :], with_ctx=ctx_out, alpha=alpha)
        ntiles = nlt + 1 if ctx_out else nlt
        x_main = _conv_ffn(h2, x1, g2, ffn_w_up[i].astype(BF16), ffn_conv_w[i], ffn_conv_b[i][None, :],
                           ffn_w_down[i].astype(BF16), ln2_g[i][None, :], ln2_b[i][None, :],
                           nlat_tiles=nlt, ntiles=ntiles, alpha=alpha)
        x_ctx = None
    return x_main[None]
```
